```python
import jax, jax.numpy as jnp
from jax import lax
import numpy as np


D_MODEL = 2048
BATCH = 2
SEQ = 4096
DEPTH = 4
DEC_BATCH = 8
DEC_SEQ = 1
PAST_LEN = 16384
PAGE_SIZE = 128

HEAD_DIM = 128
H_A = 8
W_A = H_A * HEAD_DIM
Q_BLOCK = 128
H_B = 4
DK_B = 128
DV_B = 256
WQK_B = H_B * DK_B
WV_B = H_B * DV_B
RET_CHUNK = 128
ROPE_BASE = 10000.0
CHUNK = 128
G_C = 8
CG_C = 128
W_C = G_C * CG_C
D_FF = 5632
ALPHA = (2 * DEPTH) ** 0.25
BETA = (8 * DEPTH) ** -0.25
LN_EPS = 1e-5
IN_SIZES = (W_A, W_A, W_A, H_A, WQK_B, WQK_B, WV_B, WV_B, W_C, W_C, D_MODEL, D_MODEL, D_MODEL)
N_IN = sum(IN_SIZES)

kernel_name = 'fox_retnet_gmlp_gated_hybrid_step'


def split_points():
    pts, acc = [], 0
    for s in IN_SIZES[:-1]:
        acc += s
        pts.append(acc)
    return pts


def layer_norm(x, g, b):
    xf = x.astype(jnp.float32)
    mu = xf.mean(-1, keepdims=True)
    var = jnp.square(xf - mu).mean(-1, keepdims=True)
    return ((xf - mu) * lax.rsqrt(var + LN_EPS) * g.astype(jnp.float32) + b.astype(jnp.float32)).astype(x.dtype)


def group_norm(o, g, b):
    mu = o.mean(-1, keepdims=True)
    var = jnp.square(o - mu).mean(-1, keepdims=True)
    return (o - mu) * lax.rsqrt(var + LN_EPS) * g.astype(jnp.float32) + b.astype(jnp.float32)


def swiglu(x, w_gu, w_down):
    gate, up = jnp.split(x @ w_gu, 2, axis=-1)
    return (jax.nn.silu(gate) * up) @ w_down


def rotary(x, pos):
    half = x.shape[-1] // 2
    inv = ROPE_BASE ** (-jnp.arange(half, dtype=jnp.float32) / half)
    ang = pos.astype(jnp.float32)[:, None] * inv[None, :]
    cos = jnp.cos(ang)[None, :, None, :]
    sin = jnp.sin(ang)[None, :, None, :]
    xf = x.astype(jnp.float32)
    x1, x2 = xf[..., :half], xf[..., half:]
    return jnp.concatenate([x1 * cos - x2 * sin, x1 * sin + x2 * cos], axis=-1).astype(x.dtype)


def in_projection(x, pos, w_in, b_f, vln_g, vln_b):
    B, T, _ = x.shape
    z = x @ w_in
    qa, ka, va, fa, qb, kb, vb, gb, uc, vc, ga, gbr, gc = jnp.split(z, split_points(), axis=-1)
    heads_a = lambda a: a.reshape(B, T, H_A, HEAD_DIM)
    logf = jax.nn.log_sigmoid(fa.astype(jnp.float32) + b_f.astype(jnp.float32))
    qb = rotary(qb.reshape(B, T, H_B, DK_B), pos)
    kb = rotary(kb.reshape(B, T, H_B, DK_B), pos) * (DK_B ** -0.5)
    vb = vb.reshape(B, T, H_B, DV_B)
    uc = jax.nn.gelu(uc).reshape(B, T, G_C, CG_C)
    vn = layer_norm(jax.nn.gelu(vc), vln_g, vln_b).reshape(B, T, G_C, CG_C)
    return (heads_a(qa), heads_a(ka), heads_a(va), logf, qb, kb, vb, gb, uc, vn, ga, gbr, gc)


def fox_attend(q, k, v, cq, ck, qpos, kpos):
    s = jnp.einsum('bqhd,bkhd->bhqk', q, k, preferred_element_type=jnp.float32) * (HEAD_DIM ** -0.5)
    s = s + jnp.swapaxes(cq, 1, 2)[:, :, :, None] - jnp.swapaxes(ck, 1, 2)[:, :, None, :]
    s = jnp.where(kpos[None, :] <= qpos[:, None], s, -jnp.inf)
    p = jax.nn.softmax(s, axis=-1)
    return jnp.einsum('bhqk,bkhd->bqhd', p.astype(v.dtype), v)


def fox_prompt(q, k, v, logf):
    B, S, H, D = q.shape
    c = jnp.cumsum(logf, axis=1)
    nb = S // Q_BLOCK
    qb = jnp.swapaxes(q.reshape(B, nb, Q_BLOCK, H, D), 0, 1)
    cb = jnp.swapaxes(c.reshape(B, nb, Q_BLOCK, H), 0, 1)
    kpos = jnp.arange(S)

    def block(args):
        i, qi, ci = args
        qpos = i * Q_BLOCK + jnp.arange(Q_BLOCK)
        return fox_attend(qi, k, v, ci, c, qpos, kpos)

    o = lax.map(block, (jnp.arange(nb), qb, cb))
    return jnp.swapaxes(o, 0, 1).reshape(B, S, H, D)


def fox_sample(q, k_new, v_new, logf_new, k_past, v_past, logf_past):
    P, T = k_past.shape[1], q.shape[1]
    k = jnp.concatenate([k_past, k_new.astype(k_past.dtype)], axis=1)
    v = jnp.concatenate([v_past, v_new.astype(v_past.dtype)], axis=1)
    c = jnp.cumsum(jnp.concatenate([logf_past.astype(jnp.float32), logf_new], axis=1), axis=1)
    return fox_attend(q, k, v, c[:, P:], c, P + jnp.arange(T), jnp.arange(P + T))


def retention_chunk(state, q, k, v, log_gamma):
    L = q.shape[1]
    n = jnp.arange(L, dtype=jnp.float32)
    diff = n[:, None] - n[None, :]
    dmask = jnp.where(diff >= 0, jnp.exp(log_gamma[:, None, None] * jnp.maximum(diff, 0.0)), 0.0)
    qf, kf, vf = q.astype(jnp.float32), k.astype(jnp.float32), v.astype(jnp.float32)
    a = jnp.einsum('blhk,bmhk->bhlm', qf, kf) * dmask[None]
    o_intra = jnp.einsum('bhlm,bmhv->blhv', a, vf)
    o_cross = jnp.einsum('blhk,bhkv->blhv', qf, state) * jnp.exp(log_gamma[None, :] * (n + 1.0)[:, None])[None, :, :, None]
    kdec = kf * jnp.exp(log_gamma[None, :] * (L - 1.0 - n)[:, None])[None, :, :, None]
    new_state = state * jnp.exp(log_gamma * L)[None, :, None, None] + jnp.einsum('blhk,blhv->bhkv', kdec, vf)
    return new_state, o_intra + o_cross


def retention_prompt(q, k, v, log_gamma):
    B, S = q.shape[:2]
    nc = S // RET_CHUNK
    to_chunks = lambda a: jnp.swapaxes(a.reshape(B, nc, RET_CHUNK, *a.shape[2:]), 0, 1)
    init = jnp.zeros((B, H_B, DK_B, DV_B), jnp.float32)

    def step(state, qkv):
        return retention_chunk(state, qkv[0], qkv[1], qkv[2], log_gamma)

    state, o = lax.scan(step, init, (to_chunks(q), to_chunks(k), to_chunks(v)))
    return state, jnp.swapaxes(o, 0, 1).reshape(B, S, H_B, DV_B)


def chunk_mix(vc, w_s, b_s):
    L = vc.shape[2]
    ws = jnp.tril(w_s[:, :L, :L])
    return jnp.einsum('gts,bnsgc->bntgc', ws, vc) + jnp.swapaxes(b_s[:, :L], 0, 1)[None, None, :, :, None]


def out_projection(o_a, o_b, gb, uc, s_c, ga, gbr, gc, w_up_a, w_up_b, w_up_c, w_o, gn_g, gn_b):
    B, T = o_a.shape[:2]
    dt = o_a.dtype
    ya = o_a.reshape(B, T, W_A) @ w_up_a
    yb = (jax.nn.silu(gb) * group_norm(o_b, gn_g, gn_b).reshape(B, T, WV_B).astype(dt)) @ w_up_b
    yc = (uc * s_c.astype(dt)).reshape(B, T, W_C) @ w_up_c
    y = jax.nn.sigmoid(ga) * ya + jax.nn.sigmoid(gbr) * yb + jax.nn.sigmoid(gc) * yc
    return y @ w_o


def setup_inputs(seed: int = 0) -> dict:
    key = jax.random.key(seed)
    ks = jax.random.split(key, 32)
    f32 = jnp.float32
    nrm = lambda k, shape, s=1.0: jax.random.normal(k, shape, f32) * s
    n_pages = PAST_LEN // PAGE_SIZE
    n_used = DEC_BATCH * n_pages
    n_pool = n_used + n_used // 4
    fbias = jnp.linspace(1.0, 6.0, H_A, dtype=f32)
    col_scale = jnp.concatenate([jnp.full((n,), BETA if i in (2, 6) else 1.0, f32) for i, n in enumerate(IN_SIZES)])
    gain = lambda k, shape: 1.0 + nrm(k, shape, 0.02)
    return {
        'x_prompt': nrm(ks[0], (BATCH, SEQ, D_MODEL)),
        'x_sample': nrm(ks[1], (DEC_BATCH, DEC_SEQ, D_MODEL)),
        'cache_k': nrm(ks[2], (DEPTH, n_pool, PAGE_SIZE, H_A, HEAD_DIM)),
        'cache_v': nrm(ks[3], (DEPTH, n_pool, PAGE_SIZE, H_A, HEAD_DIM), BETA),
        'cache_logf': jax.nn.log_sigmoid(fbias + nrm(ks[4], (DEPTH, n_pool, PAGE_SIZE, H_A))),
        'state_ret': nrm(ks[5], (DEPTH, DEC_BATCH, H_B, DK_B, DV_B), 0.5),
        'page_table': jax.random.permutation(ks[6], n_pool)[:n_used].reshape(DEC_BATCH, n_pages).astype(jnp.int32),
        'w_in': nrm(ks[7], (DEPTH, D_MODEL, N_IN), D_MODEL ** -0.5) * col_scale,
        'b_forget': fbias + nrm(ks[8], (DEPTH, H_A), 0.1),
        'w_up_a': nrm(ks[9], (DEPTH, W_A, D_MODEL), BETA * W_A ** -0.5),
        'w_up_b': nrm(ks[10], (DEPTH, WV_B, D_MODEL), BETA * WV_B ** -0.5),
        'w_up_c': nrm(ks[11], (DEPTH, W_C, D_MODEL), BETA * W_C ** -0.5),
        'w_out': nrm(ks[12], (DEPTH, D_MODEL, D_MODEL), BETA * D_MODEL ** -0.5),
        'ret_gn_g': gain(ks[13], (DEPTH, H_B, DV_B)),
        'ret_gn_b': nrm(ks[14], (DEPTH, H_B, DV_B), 0.02),
        'vln_g': gain(ks[15], (DEPTH, W_C)),
        'vln_b': nrm(ks[16], (DEPTH, W_C), 0.02),
        'w_spatial': nrm(ks[17], (DEPTH, G_C, CHUNK, CHUNK), 0.5 * CHUNK ** -0.5),
        'b_spatial': gain(ks[18], (DEPTH, G_C, CHUNK)),
        'ffa_gu': nrm(ks[19], (DEPTH, D_MODEL, 2 * D_FF), D_MODEL ** -0.5),
        'ffa_down': nrm(ks[20], (DEPTH, D_FF, D_MODEL), BETA * D_FF ** -0.5),
        'ffb_gu': nrm(ks[21], (DEPTH, D_MODEL, 2 * D_FF), D_MODEL ** -0.5),
        'ffb_down': nrm(ks[22], (DEPTH, D_FF, D_MODEL), BETA * D_FF ** -0.5),
        'ln_g': gain(ks[23], (DEPTH, 3, D_MODEL)),
        'ln_b': nrm(ks[24], (DEPTH, 3, D_MODEL), 0.02),
    }


def reference(x_prompt, x_sample, cache_k, cache_v, cache_logf, state_ret, page_table,
              w_in, b_forget, w_up_a, w_up_b, w_up_c, w_out, ret_gn_g, ret_gn_b,
              vln_g, vln_b, w_spatial, b_spatial, ffa_gu, ffa_down, ffb_gu, ffb_down, ln_g, ln_b):
    n_pages = PAST_LEN // PAGE_SIZE
    pos_p = jnp.arange(SEQ)
    pos_s = PAST_LEN + jnp.arange(DEC_SEQ)
    log_gamma = jnp.log1p(-jnp.exp2(-5.0 - jnp.arange(H_B, dtype=jnp.float32)))
    xp, xs = x_prompt, x_sample
    kp_l, vp_l, fp_l, sp_l, ks_l, vs_l, fs_l, ss_l, cs_l = [], [], [], [], [], [], [], [], []
    for l in range(DEPTH):
        post = lambda x, sub, j: layer_norm(ALPHA * x + sub, ln_g[l, j], ln_b[l, j])
        out_w = (w_up_a[l], w_up_b[l], w_up_c[l], w_out[l], ret_gn_g[l], ret_gn_b[l])
        xp = post(xp, 0.5 * swiglu(xp, ffa_gu[l], ffa_down[l]), 0)
        xs = post(xs, 0.5 * swiglu(xs, ffa_gu[l], ffa_down[l]), 0)
        qa, ka, va, lf, qb, kb, vb, gb, uc, vn, ga, gbr, gc = in_projection(xp, pos_p, w_in[l], b_forget[l], vln_g[l], vln_b[l])
        o_a = fox_prompt(qa, ka, va, lf)
        st_p, o_b = retention_prompt(qb, kb, vb, log_gamma)
        Bp = xp.shape[0]
        s_c = chunk_mix(vn.reshape(Bp, SEQ // CHUNK, CHUNK, G_C, CG_C), w_spatial[l], b_spatial[l]).reshape(Bp, SEQ, G_C, CG_C)
        xp = post(xp, out_projection(o_a, o_b, gb, uc, s_c, ga, gbr, gc, *out_w), 1)
        kp_l.append(ka); vp_l.append(va); fp_l.append(lf); sp_l.append(st_p)
        qa, ka, va, lf, qb, kb, vb, gb, uc, vn, ga, gbr, gc = in_projection(xs, pos_s, w_in[l], b_forget[l], vln_g[l], vln_b[l])
        Bd = xs.shape[0]
        k_past = cache_k[l][page_table].reshape(Bd, n_pages * PAGE_SIZE, H_A, HEAD_DIM)
        v_past = cache_v[l][page_table].reshape(Bd, n_pages * PAGE_SIZE, H_A, HEAD_DIM)
        f_past = cache_logf[l][page_table].reshape(Bd, n_pages * PAGE_SIZE, H_A)
        o_a = fox_sample(qa, ka, va, lf, k_past, v_past, f_past)
        st_s, o_b = retention_chunk(state_ret[l].astype(jnp.float32), qb, kb, vb, log_gamma)
        s_c = chunk_mix(vn[:, None], w_spatial[l], b_spatial[l])[:, 0]
        xs = post(xs, out_projection(o_a, o_b, gb, uc, s_c, ga, gbr, gc, *out_w), 1)
        ks_l.append(ka); vs_l.append(va); fs_l.append(lf); ss_l.append(st_s); cs_l.append(vn)
        xp = post(xp, 0.5 * swiglu(xp, ffb_gu[l], ffb_down[l]), 2)
        xs = post(xs, 0.5 * swiglu(xs, ffb_gu[l], ffb_down[l]), 2)
    return (xp, xs,
            jnp.stack(kp_l), jnp.stack(vp_l), jnp.stack(fp_l), jnp.stack(sp_l),
            jnp.stack(ks_l), jnp.stack(vs_l), jnp.stack(fs_l), jnp.stack(ss_l), jnp.stack(cs_l))
```

```python
import functools
import math

import jax
import jax.numpy as jnp
from jax import lax
from jax.experimental import pallas as pl
from jax.experimental.pallas import tpu as pltpu

F32 = jnp.float32
BF16 = jnp.bfloat16

LN_EPS = 1e-5
ROPE_BASE = 10000.0
NEG_BIG = -1e30
LANES = 128
SAMPLE_ROWS = 16
V7X_VMEM_LIMIT = 56 * 1024 * 1024


def _pick_tile(n, max_tile, mult):
    best = None
    for t in range(mult, min(n, max_tile) + 1, mult):
        if n % t == 0:
            best = t
    assert best is not None, (n, max_tile, mult)
    return best


def _params(sem):
    return pltpu.CompilerParams(dimension_semantics=sem, vmem_limit_bytes=V7X_VMEM_LIMIT)


def _layer_norm(y, g, b):
    mu = jnp.mean(y, axis=-1, keepdims=True)
    d = y - mu
    var = jnp.mean(d * d, axis=-1, keepdims=True)
    return d * lax.rsqrt(var + LN_EPS) * g + b


def _silu(x):
    return x * jax.nn.sigmoid(x)


def _gelu_tanh(x):
    c = math.sqrt(2.0 / math.pi)
    return 0.5 * x * (1.0 + jnp.tanh(c * (x + 0.044715 * (x * x * x))))


def _log_sigmoid(x):
    return jnp.minimum(x, 0.0) - jnp.log1p(jnp.exp(-jnp.abs(x)))


def _split3(x):
    hi = x.astype(BF16)
    r = x - hi.astype(F32)
    mid = r.astype(BF16)
    lo = (r - mid.astype(F32)).astype(BF16)
    return hi, mid, lo


def _ffn_kernel(x_ref, wg_ref, wu_ref, wd_ref, g_ref, b_ref, o_ref, xb_ref, *, alpha, nf):
    f = pl.program_id(1)

    @pl.when(f == 0)
    def _():
        xb_ref[...] = x_ref[...].astype(BF16)
        o_ref[...] = jnp.zeros_like(o_ref)

    xb = xb_ref[...]
    hg = jnp.dot(xb, wg_ref[...], preferred_element_type=F32)
    hu = jnp.dot(xb, wu_ref[...], preferred_element_type=F32)
    a = (_silu(hg) * hu).astype(BF16)
    o_ref[...] += jnp.dot(a, wd_ref[...], preferred_element_type=F32)

    @pl.when(f == nf - 1)
    def _():
        o_ref[...] = _layer_norm(alpha * x_ref[...] + 0.5 * o_ref[...], g_ref[...], b_ref[...])


def _ffn(x, w_gu, w_down, ln_g, ln_b, l, ln_idx, *, alpha, tm):
    mt, d = x.shape
    dff = w_down.shape[1]
    tf = _pick_tile(dff, 512, LANES)
    nf = dff // tf
    return pl.pallas_call(
        functools.partial(_ffn_kernel, alpha=alpha, nf=nf),
        grid=(mt // tm, nf),
        in_specs=[
            pl.BlockSpec((tm, d), lambda m, f: (m, 0), pipeline_mode=pl.Buffered(1)),
            pl.BlockSpec((None, d, tf), lambda m, f: (l, 0, f)),
            pl.BlockSpec((None, d, tf), lambda m, f: (l, 0, f + nf)),
            pl.BlockSpec((None, tf, d), lambda m, f: (l, f, 0)),
            pl.BlockSpec((None, 1, d), lambda m, f: (ln_idx, 0, 0)),
            pl.BlockSpec((None, 1, d), lambda m, f: (ln_idx, 0, 0)),
        ],
        out_specs=pl.BlockSpec((tm, d), lambda m, f: (m, 0)),
        out_shape=jax.ShapeDtypeStruct((mt, d), F32),
        scratch_shapes=[pltpu.VMEM((tm, d), BF16)],
        compiler_params=_params(("parallel", "arbitrary")),
        name="ffn",
    )(x, w_gu, w_gu, w_down, ln_g, ln_b)


def _proj_a_kernel(x_ref, w_ref, wf_ref, bf_ref, z_ref, lf_ref, xb_ref):
    @pl.when(pl.program_id(1) == 0)
    def _():
        xb0 = x_ref[...].astype(BF16)
        xb_ref[...] = xb0
        fa = jnp.dot(xb0, wf_ref[...], preferred_element_type=F32) + bf_ref[...]
        lf_ref[...] = _log_sigmoid(fa)

    z_ref[...] = jnp.dot(xb_ref[...], w_ref[...], preferred_element_type=F32)


def _proj_a(x, w_main, w_f, b_f, l, *, ncols, tm):
    mt, d = x.shape
    tn = _pick_tile(ncols, 512, LANES)
    return pl.pallas_call(
        _proj_a_kernel,
        grid=(mt // tm, ncols // tn),
        in_specs=[
            pl.BlockSpec((tm, d), lambda m, n: (m, 0)),
            pl.BlockSpec((None, d, tn), lambda m, n: (l, 0, n)),
            pl.BlockSpec((None, d, LANES), lambda m, n: (l, 0, 0)),
            pl.BlockSpec((None, 1, LANES), lambda m, n: (l, 0, 0)),
        ],
        out_specs=[
            pl.BlockSpec((tm, tn), lambda m, n: (m, n)),
            pl.BlockSpec((tm, LANES), lambda m, n: (m, 0)),
            pl.BlockSpec((tm, d), lambda m, n: (m, 0)),
        ],
        out_shape=[jax.ShapeDtypeStruct((mt, ncols), F32), jax.ShapeDtypeStruct((mt, LANES), F32),
                   jax.ShapeDtypeStruct((mt, d), BF16)],
        compiler_params=_params(("parallel", "arbitrary")),
        name="proj_a",
    )(x, w_main, w_f, b_f)


def _proj_b_kernel(xb_ref, w_ref, cos_ref, sin_ref, z_ref, *, n_qk, n_v, dk, kscale):
    n = pl.program_id(1)
    acc = jnp.dot(xb_ref[...], w_ref[...], preferred_element_type=F32)
    tn = acc.shape[1]

    @pl.when(n < 2 * n_qk)
    def _():
        cos = cos_ref[...]
        sin = sin_ref[...]
        scale = jnp.where(n >= n_qk, kscale, 1.0).astype(F32)
        for h in range(tn // dk):
            xh = acc[:, h * dk:(h + 1) * dk]
            r = xh * cos + pltpu.roll(xh, dk // 2, 1) * sin
            z_ref[:, h * dk:(h + 1) * dk] = (r * scale).astype(BF16)

    @pl.when(jnp.logical_and(n >= 2 * n_qk, n < 2 * n_qk + n_v))
    def _():
        z_ref[...] = acc.astype(BF16)

    @pl.when(n >= 2 * n_qk + n_v)
    def _():
        z_ref[...] = _silu(acc).astype(BF16)


def _proj_b(xb, w_main, cos_t, sin_t, l, *, col0, wqk, wv, dk, tm):
    mt, d = xb.shape
    assert dk == LANES
    tn = _pick_tile(wqk, 512, dk)
    assert wv % tn == 0 and col0 % tn == 0
    ncols = 2 * wqk + 2 * wv
    kern = functools.partial(_proj_b_kernel, n_qk=wqk // tn, n_v=wv // tn, dk=dk, kscale=dk ** -0.5)
    return pl.pallas_call(
        kern,
        grid=(mt // tm, ncols // tn),
        in_specs=[
            pl.BlockSpec((tm, d), lambda m, n: (m, 0)),
            pl.BlockSpec((None, d, tn), lambda m, n: (l, 0, n + col0 // tn)),
            pl.BlockSpec((tm, dk), lambda m, n: (m, 0)),
            pl.BlockSpec((tm, dk), lambda m, n: (m, 0)),
        ],
        out_specs=pl.BlockSpec((tm, tn), lambda m, n: (m, n)),
        out_shape=jax.ShapeDtypeStruct((mt, ncols), BF16),
        compiler_params=_params(("parallel", "arbitrary")),
        name="proj_b",
    )(xb, w_main, cos_t, sin_t)


def _proj_c_kernel(xb_ref, w_ref, g_ref, b_ref, u_ref, vn_ref):
    n = pl.program_id(1)
    acc = _gelu_tanh(jnp.dot(xb_ref[...], w_ref[...], preferred_element_type=F32))

    @pl.when(n == 0)
    def _():
        u_ref[...] = acc.astype(BF16)

    @pl.when(n == 1)
    def _():
        vn_ref[...] = _layer_norm(acc, g_ref[...], b_ref[...])


def _proj_c(xb, w_main, vln_g, vln_b, l, *, col0, wc, tm):
    mt, d = xb.shape
    assert col0 % wc == 0
    return pl.pallas_call(
        _proj_c_kernel,
        grid=(mt // tm, 2),
        in_specs=[
            pl.BlockSpec((tm, d), lambda m, n: (m, 0)),
            pl.BlockSpec((None, d, wc), lambda m, n: (l, 0, n + col0 // wc)),
            pl.BlockSpec((None, 1, wc), lambda m, n: (l, 0, 0)),
            pl.BlockSpec((None, 1, wc), lambda m, n: (l, 0, 0)),
        ],
        out_specs=[
            pl.BlockSpec((tm, wc), lambda m, n: (m, 0)),
            pl.BlockSpec((tm, wc), lambda m, n: (m, 0)),
        ],
        out_shape=[jax.ShapeDtypeStruct((mt, wc), BF16), jax.ShapeDtypeStruct((mt, wc), F32)],
        compiler_params=_params(("parallel", "arbitrary")),
        name="proj_c",
    )(xb, w_main, vln_g, vln_b)


def _proj_gate_kernel(xb_ref, w_ref, z_ref):
    acc = jnp.dot(xb_ref[...], w_ref[...], preferred_element_type=F32)
    z_ref[...] = jax.nn.sigmoid(acc).astype(BF16)


def _proj_gate(xb, w_main, l, *, col0, ncols, tm):
    mt, d = xb.shape
    tn = _pick_tile(math.gcd(ncols, col0), 512, LANES)
    return pl.pallas_call(
        _proj_gate_kernel,
        grid=(mt // tm, ncols // tn),
        in_specs=[
            pl.BlockSpec((tm, d), lambda m, n: (m, 0)),
            pl.BlockSpec((None, d, tn), lambda m, n: (l, 0, n + col0 // tn)),
        ],
        out_specs=pl.BlockSpec((tm, tn), lambda m, n: (m, n)),
        out_shape=jax.ShapeDtypeStruct((mt, ncols), BF16),
        compiler_params=_params(("parallel", "arbitrary")),
        name="proj_gate",
    )(xb, w_main)


def _rope_kernel(pos_ref, cos_ref, sin_ref, *, half):
    lane = lax.broadcasted_iota(jnp.int32, pos_ref.shape, 1)
    j = jnp.where(lane >= half, lane - half, lane).astype(F32)
    inv = jnp.exp(j * (-math.log(ROPE_BASE) / half))
    ang = pos_ref[...] * inv
    cos_ref[...] = jnp.cos(ang)
    sin_ref[...] = jnp.where(lane >= half, 1.0, -1.0) * jnp.sin(ang)


def _rope_tables(pos_b, *, tm):
    mt, dk = pos_b.shape
    spec = pl.BlockSpec((tm, dk), lambda m: (m, 0))
    return pl.pallas_call(
        functools.partial(_rope_kernel, half=dk // 2),
        grid=(mt // tm,),
        in_specs=[spec],
        out_specs=[spec, spec],
        out_shape=[jax.ShapeDtypeStruct((mt, dk), F32)] * 2,
        compiler_params=_params(("parallel",)),
        name="rope_tables",
    )(pos_b)


def _cumsum_kernel(lf_ref, c_ref, carry_ref):
    @pl.when(pl.program_id(1) == 0)
    def _():
        carry_ref[...] = jnp.zeros_like(carry_ref)

    x = lf_ref[...]
    tc = x.shape[0]
    row = lax.broadcasted_iota(jnp.int32, (tc, tc), 0)
    col = lax.broadcasted_iota(jnp.int32, (tc, tc), 1)
    tri = jnp.where(col <= row, 1.0, 0.0).astype(BF16)
    c = carry_ref[0:1, :]
    for part in _split3(x):
        c = c + jnp.dot(tri, part, preferred_element_type=F32)
    c_ref[...] = c
    carry_ref[...] = jnp.broadcast_to(c[tc - 1:tc, :], carry_ref.shape)


def _cumsum(lf, *, batch, seq):
    tc = _pick_tile(seq, 512, 8)
    nt = seq // tc
    return pl.pallas_call(
        _cumsum_kernel,
        grid=(batch, nt),
        in_specs=[pl.BlockSpec((tc, LANES), lambda b, t: (b * nt + t, 0))],
        out_specs=pl.BlockSpec((tc, LANES), lambda b, t: (b * nt + t, 0)),
        out_shape=jax.ShapeDtypeStruct((batch * seq, LANES), F32),
        scratch_shapes=[pltpu.VMEM((8, LANES), F32)],
        compiler_params=_params(("parallel", "arbitrary")),
        name="logf_cumsum",
    )(lf)


def _fox_prompt_kernel(q_ref, k_ref, v_ref, ck_ref, o_ref, kb_ref, vb_ref, m_ref, l_ref, acc_ref, *, tq, scale):
    i = pl.program_id(2)

    @pl.when(i == 0)
    def _():
        kb_ref[...] = k_ref[...].astype(BF16)
        vb_ref[...] = v_ref[...].astype(BF16)

    q = (q_ref[...] * scale).astype(BF16)
    m_ref[...] = jnp.full_like(m_ref, NEG_BIG)
    l_ref[...] = jnp.zeros_like(l_ref)
    acc_ref[...] = jnp.zeros_like(acc_ref)

    def step(j, masked):
        off = pl.multiple_of(j * tq, tq)
        k = kb_ref[pl.ds(off, tq), :]
        v = vb_ref[pl.ds(off, tq), :]
        s = lax.dot_general(q, k, (((1,), (1,)), ((), ())), preferred_element_type=F32) - ck_ref[j]
        if masked:
            row = lax.broadcasted_iota(jnp.int32, s.shape, 0)
            col = lax.broadcasted_iota(jnp.int32, s.shape, 1)
            s = jnp.where(col <= row, s, NEG_BIG)
        m_prev = m_ref[...]
        m_new = jnp.maximum(m_prev, jnp.max(s, axis=1, keepdims=True))
        p = jnp.exp(s - m_new)
        alpha = jnp.exp(m_prev - m_new)
        l_ref[...] = alpha * l_ref[...] + jnp.sum(p, axis=1, keepdims=True)
        acc_ref[...] = alpha * acc_ref[...] + jnp.dot(p.astype(BF16), v, preferred_element_type=F32)
        m_ref[...] = m_new

    def body(j, carry):
        step(j, False)
        return carry

    lax.fori_loop(0, i, body, 0)
    step(i, True)
    o_ref[...] = (acc_ref[...] / l_ref[...]).astype(BF16)


def _fox_prompt(z_a, ck, *, batch, seq, heads, hd):
    assert hd == LANES
    tq = _pick_tile(seq, 512, LANES)
    nq = seq // tq
    kern = functools.partial(_fox_prompt_kernel, tq=tq, scale=hd ** -0.5)
    return pl.pallas_call(
        kern,
        grid=(batch, heads, nq),
        in_specs=[
            pl.BlockSpec((tq, hd), lambda b, h, i: (b * nq + i, h)),
            pl.BlockSpec((seq, hd), lambda b, h, i: (b, heads + h)),
            pl.BlockSpec((seq, hd), lambda b, h, i: (b, 2 * heads + h)),
            pl.BlockSpec((None, nq, 1, tq), lambda b, h, i: (b * heads + h, 0, 0, 0)),
        ],
        out_specs=pl.BlockSpec((tq, hd), lambda b, h, i: (b * nq + i, h)),
        out_shape=jax.ShapeDtypeStruct((batch * seq, heads * hd), BF16),
        scratch_shapes=[
            pltpu.VMEM((seq, hd), BF16),
            pltpu.VMEM((seq, hd), BF16),
            pltpu.VMEM((tq, 1), F32),
            pltpu.VMEM((tq, 1), F32),
            pltpu.VMEM((tq, hd), F32),
        ],
        compiler_params=_params(("parallel", "parallel", "arbitrary")),
        name="fox_prompt",
    )(z_a, z_a, z_a, ck)


def _fox_sample_kernel(pt_ref, q_ref, kn_ref, vn_ref, lfn_ref, kp_ref, vp_ref, lft_ref, o_ref,
                       qbd_ref, m_ref, l_ref, acc_ref, carry_ref, *, heads, hd, scale, n_pages):
    del pt_ref
    b = pl.program_id(0)
    j = pl.program_id(1)
    w = heads * hd
    lane_head = lax.broadcasted_iota(jnp.int32, (heads, w), 1) // hd
    diag = lane_head == lax.broadcasted_iota(jnp.int32, (heads, w), 0)

    @pl.when(j == 0)
    def _():
        @pl.when(b == 0)
        def _():
            o_ref[...] = jnp.zeros_like(o_ref)

        qrow = q_ref[pl.ds(b, 1), :] * scale
        qbd = jnp.where(diag, jnp.broadcast_to(qrow, (heads, w)), 0.0)
        qbd_ref[...] = qbd.astype(BF16)
        m_ref[...] = jnp.sum(qbd * kn_ref[pl.ds(b, 1), :], axis=1, keepdims=True)
        l_ref[...] = jnp.ones_like(l_ref)
        acc_ref[...] = jnp.broadcast_to(vn_ref[pl.ds(b, 1), :], (heads, w))
        eye = (lax.broadcasted_iota(jnp.int32, (heads, LANES), 0)
               == lax.broadcasted_iota(jnp.int32, (heads, LANES), 1))
        lfrow = jnp.broadcast_to(lfn_ref[pl.ds(b, 1), :], (heads, LANES))
        carry_ref[...] = jnp.sum(jnp.where(eye, lfrow, 0.0), axis=1, keepdims=True)

    k = kp_ref[...].astype(BF16)
    v = vp_ref[...].astype(BF16)
    lft = lft_ref[...]
    page = lft.shape[1]
    s = lax.dot_general(qbd_ref[...], k, (((1,), (1,)), ((), ())), preferred_element_type=F32)
    later = (lax.broadcasted_iota(jnp.int32, (page, page), 0)
             > lax.broadcasted_iota(jnp.int32, (page, page), 1))
    later = jnp.where(later, 1.0, 0.0).astype(BF16)
    bias = carry_ref[...]
    for part in _split3(lft):
        bias = bias + jnp.dot(part, later, preferred_element_type=F32)
    s = s + bias
    m_prev = m_ref[...]
    m_new = jnp.maximum(m_prev, jnp.max(s, axis=1, keepdims=True))
    p = jnp.exp(s - m_new)
    alpha = jnp.exp(m_prev - m_new)
    l_ref[...] = alpha * l_ref[...] + jnp.sum(p, axis=1, keepdims=True)
    acc_ref[...] = alpha * acc_ref[...] + jnp.dot(p.astype(BF16), v, preferred_element_type=F32)
    m_ref[...] = m_new
    carry_ref[...] = carry_ref[...] + jnp.sum(lft, axis=1, keepdims=True)

    @pl.when(j == n_pages - 1)
    def _():
        o = jnp.where(diag, acc_ref[...] / l_ref[...], 0.0)
        o_ref[pl.ds(b, 1), :] = jnp.sum(o, axis=0, keepdims=True)


def _fox_sample(z_a, lf, cache_k, cache_v, cache_lft, page_table, l, *, row0, dec_batch, heads, hd):
    w = heads * hd
    n_pages = page_table.shape[1]
    page = cache_k.shape[2]
    rb = row0 // SAMPLE_ROWS
    kern = functools.partial(_fox_sample_kernel, heads=heads, hd=hd, scale=hd ** -0.5, n_pages=n_pages)

    def page_idx(b, j, pt):
        return (l, pt[b, n_pages - 1 - j], 0, 0)

    grid_spec = pltpu.PrefetchScalarGridSpec(
        num_scalar_prefetch=1,
        grid=(dec_batch, n_pages),
        in_specs=[
            pl.BlockSpec((SAMPLE_ROWS, w), lambda b, j, pt: (rb, 0)),
            pl.BlockSpec((SAMPLE_ROWS, w), lambda b, j, pt: (rb, 1)),
            pl.BlockSpec((SAMPLE_ROWS, w), lambda b, j, pt: (rb, 2)),
            pl.BlockSpec((SAMPLE_ROWS, LANES), lambda b, j, pt: (rb, 0)),
            pl.BlockSpec((None, None, page, w), page_idx),
            pl.BlockSpec((None, None, page, w), page_idx),
            pl.BlockSpec((None, None, heads, page), page_idx),
        ],
        out_specs=pl.BlockSpec((SAMPLE_ROWS, w), lambda b, j, pt: (0, 0)),
        scratch_shapes=[
            pltpu.VMEM((heads, w), BF16),
            pltpu.VMEM((heads, 1), F32),
            pltpu.VMEM((heads, 1), F32),
            pltpu.VMEM((heads, w), F32),
            pltpu.VMEM((heads, 1), F32),
        ],
    )
    return pl.pallas_call(
        kern,
        grid_spec=grid_spec,
        out_shape=jax.ShapeDtypeStruct((SAMPLE_ROWS, w), F32),
        compiler_params=_params(("arbitrary", "arbitrary")),
        name="fox_sample",
    )(page_table, z_a, z_a, z_a, lf, cache_k, cache_v, cache_lft)


def _log_gammas(n_heads):
    return [math.log1p(-(2.0 ** (-5.0 - h))) for h in range(n_heads)]


def _group_norm_gate(o, gate, g, b):
    mu = jnp.mean(o, axis=-1, keepdims=True)
    d = o - mu
    var = jnp.mean(d * d, axis=-1, keepdims=True)
    return gate * (d * lax.rsqrt(var + LN_EPS) * g + b)


def _ret_prompt_kernel(q_ref, k_ref, v_ref, gate_ref, g_ref, b_ref, y_ref, st_ref, *, heads, dk, dv, chunk):
    @pl.when(pl.program_id(1) == 0)
    def _():
        st_ref[...] = jnp.zeros_like(st_ref)

    n_col = lax.broadcasted_iota(jnp.int32, (chunk, 1), 0).astype(F32)
    diff = (lax.broadcasted_iota(jnp.int32, (chunk, chunk), 0)
            - lax.broadcasted_iota(jnp.int32, (chunk, chunk), 1)).astype(F32)
    for h, lg in enumerate(_log_gammas(heads)):
        q = q_ref[:, h * dk:(h + 1) * dk]
        k = k_ref[:, h * dk:(h + 1) * dk]
        v = v_ref[:, h * dv:(h + 1) * dv]
        state = st_ref[h]
        dmask = jnp.where(diff >= 0, jnp.exp(lg * jnp.maximum(diff, 0.0)), 0.0)
        a = lax.dot_general(q, k, (((1,), (1,)), ((), ())), preferred_element_type=F32) * dmask
        o = jnp.dot(a.astype(BF16), v, preferred_element_type=F32)
        o = o + jnp.dot(q, state.astype(BF16), preferred_element_type=F32) * jnp.exp(lg * (n_col + 1.0))
        kdec = k.astype(F32) * jnp.exp(lg * (chunk - 1.0 - n_col))
        st_ref[h] = state * math.exp(lg * chunk) + jnp.dot(kdec.T.astype(BF16), v, preferred_element_type=F32)
        gate = gate_ref[:, h * dv:(h + 1) * dv].astype(F32)
        y = _group_norm_gate(o, gate, g_ref[h:h + 1, :], b_ref[h:h + 1, :])
        y_ref[:, h * dv:(h + 1) * dv] = y.astype(BF16)


def _ret_prompt(z_b, gn_g, gn_b, l, *, batch, seq, heads, dk, dv):
    chunk = LANES
    nc = seq // chunk
    wqk, wv = heads * dk, heads * dv
    assert wv % wqk == 0
    kern = functools.partial(_ret_prompt_kernel, heads=heads, dk=dk, dv=dv, chunk=chunk)
    return pl.pallas_call(
        kern,
        grid=(batch, nc),
        in_specs=[
            pl.BlockSpec((chunk, wqk), lambda b, c: (b * nc + c, 0)),
            pl.BlockSpec((chunk, wqk), lambda b, c: (b * nc + c, 1)),
            pl.BlockSpec((chunk, wv), lambda b, c: (b * nc + c, 2 * wqk // wv)),
            pl.BlockSpec((chunk, wv), lambda b, c: (b * nc + c, 2 * wqk // wv + 1)),
            pl.BlockSpec((None, heads, dv), lambda b, c: (l, 0, 0)),
            pl.BlockSpec((None, heads, dv), lambda b, c: (l, 0, 0)),
        ],
        out_specs=[
            pl.BlockSpec((chunk, wv), lambda b, c: (b * nc + c, 0)),
            pl.BlockSpec((None, heads, dk, dv), lambda b, c: (b, 0, 0, 0)),
        ],
        out_shape=[
            jax.ShapeDtypeStruct((batch * seq, wv), BF16),
            jax.ShapeDtypeStruct((batch, heads, dk, dv), F32),
        ],
        compiler_params=_params(("parallel", "arbitrary")),
        name="ret_prompt",
    )(z_b, z_b, z_b, z_b, gn_g, gn_b)


def _sample_mix_kernel(zb_ref, st_ref, g_ref, b_ref, u_ref, vn_ref, ws_ref, bs_ref, y_ref, yc_ref, nst_ref,
                       *, heads, dk, dv):
    b = pl.program_id(0)

    @pl.when(b == 0)
    def _():
        y_ref[...] = jnp.zeros_like(y_ref)
        yc_ref[...] = u_ref[...].astype(F32) * (ws_ref[...] * vn_ref[...] + bs_ref[...])

    wqk, wv = heads * dk, heads * dv
    blk = zb_ref[...].astype(F32)
    pick = lax.broadcasted_iota(jnp.int32, blk.shape, 0) == b
    row = jnp.sum(jnp.where(pick, blk, 0.0), axis=0, keepdims=True)
    eye =(lax.broadcasted_iota(jnp.int32, (dk, dk), 0) == lax.broadcasted_iota(jnp.int32, (dk, dk), 1))
    for h, lg in enumerate(_log_gammas(heads)):
        q = row[:, h * dk:(h + 1) * dk]
        k = row[:, wqk + h * dk:wqk + (h + 1) * dk]
        v = row[:, 2 * wqk + h * dv:2 * wqk + (h + 1) * dv]
        gate = row[:, 2 * wqk + wv + h * dv:2 * wqk + wv + (h + 1) * dv]
        qcol = jnp.sum(jnp.where(eye, jnp.broadcast_to(q, (dk, dk)), 0.0), axis=1, keepdims=True)
        kcol = jnp.sum(jnp.where(eye, jnp.broadcast_to(k, (dk, dk)), 0.0), axis=1, keepdims=True)
        state = st_ref[h]
        gamma = math.exp(lg)
        new_state = state * gamma + kcol * v
        nst_ref[h] = new_state
        o = jnp.sum(q * k, axis=1, keepdims=True) * v + jnp.sum(qcol * state, axis=0, keepdims=True) * gamma
        y = _group_norm_gate(o, gate, g_ref[h:h + 1, :], b_ref[h:h + 1, :])
        y_ref[pl.ds(b, 1), h * dv:(h + 1) * dv] = y


def _sample_mix(z_b, state_ret, gn_g, gn_b, u_c, vn, ws_row, bs_row, l, *, row0, dec_batch, heads, dk, dv):
    wv = heads * dv
    wc = u_c.shape[1]
    rb = row0 // SAMPLE_ROWS
    kern = functools.partial(_sample_mix_kernel, heads=heads, dk=dk, dv=dv)
    return pl.pallas_call(
        kern,
        grid=(dec_batch,),
        in_specs=[
            pl.BlockSpec((SAMPLE_ROWS, z_b.shape[1]), lambda b: (rb, 0)),
            pl.BlockSpec((None, None, heads, dk, dv), lambda b: (l, b, 0, 0, 0)),
            pl.BlockSpec((None, heads, dv), lambda b: (l, 0, 0)),
            pl.BlockSpec((None, heads, dv), lambda b: (l, 0, 0)),
            pl.BlockSpec((SAMPLE_ROWS, wc), lambda b: (rb, 0)),
            pl.BlockSpec((SAMPLE_ROWS, wc), lambda b: (rb, 0)),
            pl.BlockSpec((None, 1, wc), lambda b: (l, 0, 0)),
            pl.BlockSpec((None, 1, wc), lambda b: (l, 0, 0)),
        ],
        out_specs=[
            pl.BlockSpec((SAMPLE_ROWS, wv), lambda b: (0, 0)),
            pl.BlockSpec((SAMPLE_ROWS, wc), lambda b: (0, 0)),
            pl.BlockSpec((None, heads, dk, dv), lambda b: (b, 0, 0, 0)),
        ],
        out_shape=[
            jax.ShapeDtypeStruct((SAMPLE_ROWS, wv), F32),
            jax.ShapeDtypeStruct((SAMPLE_ROWS, wc), F32),
            jax.ShapeDtypeStruct((dec_batch, heads, dk, dv), F32),
        ],
        compiler_params=_params(("arbitrary",)),
        name="sample_mix",
    )(z_b, state_ret, gn_g, gn_b, u_c, vn, ws_row, bs_row)


def _chunk_mix_kernel(vn_ref, u_ref, ws_ref, bst_ref, y_ref, *, groups, cg):
    chunk = vn_ref.shape[0]
    lower = (lax.broadcasted_iota(jnp.int32, (chunk, chunk), 1)
             <= lax.broadcasted_iota(jnp.int32, (chunk, chunk), 0))
    bst = bst_ref[...]
    for g in range(groups):
        w = jnp.where(lower, ws_ref[g], 0.0).astype(BF16)
        vg = vn_ref[:, g * cg:(g + 1) * cg].astype(BF16)
        s = jnp.dot(w, vg, preferred_element_type=F32) + bst[:, g:g + 1]
        y_ref[:, g * cg:(g + 1) * cg] = (u_ref[:, g * cg:(g + 1) * cg].astype(F32) * s).astype(BF16)


def _chunk_mix(vn, u_c, w_spatial, bs_t, l, *, rows, groups, cg):
    chunk = w_spatial.shape[2]
    wc = groups * cg
    kern = functools.partial(_chunk_mix_kernel, groups=groups, cg=cg)
    return pl.pallas_call(
        kern,
        grid=(rows // chunk,),
        in_specs=[
            pl.BlockSpec((chunk, wc), lambda n: (n, 0)),
            pl.BlockSpec((chunk, wc), lambda n: (n, 0)),
            pl.BlockSpec((None, groups, chunk, chunk), lambda n: (l, 0, 0, 0)),
            pl.BlockSpec((None, chunk, groups), lambda n: (l, 0, 0)),
        ],
        out_specs=pl.BlockSpec((chunk, wc), lambda n: (n, 0)),
        out_shape=jax.ShapeDtypeStruct((rows, wc), BF16),
        compiler_params=_params(("parallel",)),
        name="chunk_mix",
    )(vn, u_c, w_spatial, bs_t)


def _merge_kernel(oa_ref, yb_ref, yc_ref, ga_ref, gb_ref, gc_ref, wa_ref, wb_ref, wc_ref, y_ref):
    ya = jnp.dot(oa_ref[...], wa_ref[...], preferred_element_type=F32)
    yb = jnp.dot(yb_ref[...], wb_ref[...], preferred_element_type=F32)
    yc = jnp.dot(yc_ref[...], wc_ref[...], preferred_element_type=F32)
    y = ga_ref[...].astype(F32) * ya + gb_ref[...].astype(F32) * yb + gc_ref[...].astype(F32) * yc
    y_ref[...] = y.astype(BF16)


def _merge(o_a, y_b, y_c, gates, w_up_a, w_up_b, w_up_c, l, *, tm):
    mt = o_a.shape[0]
    d = w_up_a.shape[2]
    tn = _pick_tile(d, 512, LANES)
    nn = d // tn

    def act(a):
        return pl.BlockSpec((tm, a.shape[1]), lambda m, n: (m, 0))

    def wgt(w):
        return pl.BlockSpec((None, w.shape[1], tn), lambda m, n: (l, 0, n))

    return pl.pallas_call(
        _merge_kernel,
        grid=(mt // tm, nn),
        in_specs=[
            act(o_a), act(y_b), act(y_c),
            pl.BlockSpec((tm, tn), lambda m, n: (m, n)),
            pl.BlockSpec((tm, tn), lambda m, n: (m, n + nn)),
            pl.BlockSpec((tm, tn), lambda m, n: (m, n + 2 * nn)),
            wgt(w_up_a), wgt(w_up_b), wgt(w_up_c),
        ],
        out_specs=pl.BlockSpec((tm, tn), lambda m, n: (m, n)),
        out_shape=jax.ShapeDtypeStruct((mt, d), BF16),
        compiler_params=_params(("parallel", "arbitrary")),
        name="merge",
    )(o_a, y_b, y_c, gates, gates, gates, w_up_a, w_up_b, w_up_c)


def _out_kernel(x_ref, y_ref, w_ref, g_ref, b_ref, o_ref, *, alpha):
    sub = jnp.dot(y_ref[...], w_ref[...], preferred_element_type=F32)
    o_ref[...] = _layer_norm(alpha * x_ref[...] + sub, g_ref[...], b_ref[...])


def _out_proj(x, y, w_out, ln_g, ln_b, l, ln_idx, *, alpha):
    mt, d = x.shape
    tm = _pick_tile(mt, 512, SAMPLE_ROWS)
    row = pl.BlockSpec((tm, d), lambda m: (m, 0))
    return pl.pallas_call(
        functools.partial(_out_kernel, alpha=alpha),
        grid=(mt // tm,),
        in_specs=[
            row, row,
            pl.BlockSpec((None, d, d), lambda m: (l, 0, 0)),
            pl.BlockSpec((None, 1, d), lambda m: (ln_idx, 0, 0)),
            pl.BlockSpec((None, 1, d), lambda m: (ln_idx, 0, 0)),
        ],
        out_specs=row,
        out_shape=jax.ShapeDtypeStruct((mt, d), F32),
        compiler_params=_params(("parallel",)),
        name="out_proj",
    )(x, y, w_out, ln_g, ln_b)


def _with_sample_rows(prompt_rows, sample_rows):
    return jnp.concatenate([prompt_rows, sample_rows.astype(prompt_rows.dtype)], axis=0)


def kernel(x_prompt, x_sample, cache_k, cache_v, cache_logf, state_ret, page_table, w_in, b_forget, w_up_a, w_up_b, w_up_c, w_out, ret_gn_g, ret_gn_b, vln_g, vln_b, w_spatial, b_spatial, ffa_gu, ffa_down, ffb_gu, ffb_down, ln_g, ln_b):
    batch, seq, d = x_prompt.shape
    dec_batch, dec_seq, _ = x_sample.shape
    depth = w_in.shape[0]
    heads_a, hd = cache_k.shape[3], cache_k.shape[4]
    n_pool, page = cache_k.shape[1], cache_k.shape[2]
    heads_b, dk, dv = state_ret.shape[2], state_ret.shape[3], state_ret.shape[4]
    groups, chunk = w_spatial.shape[1], w_spatial.shape[2]
    wa, wqk, wv, wc = heads_a * hd, heads_b * dk, heads_b * dv, vln_g.shape[1]
    cg = wc // groups
    assert dec_seq == 1 and dec_batch <= SAMPLE_ROWS and heads_a <= LANES
    assert seq % chunk == 0 and chunk == LANES and (batch * seq) % SAMPLE_ROWS == 0
    assert w_in.shape[2] == 3 * wa + heads_a + 2 * wqk + 2 * wv + 2 * wc + 3 * d
    past_len = page_table.shape[1] * page
    alpha = (2 * depth) ** 0.25

    n_prompt = batch * seq
    mt = n_prompt + SAMPLE_ROWS
    tm = _pick_tile(mt, 1024, SAMPLE_ROWS)

    w_main = jnp.concatenate([w_in[:, :, :3 * wa], w_in[:, :, 3 * wa + heads_a:]], axis=2).astype(BF16)
    w_f = jnp.pad(w_in[:, :, 3 * wa:3 * wa + heads_a], ((0, 0), (0, 0), (0, LANES - heads_a))).astype(BF16)
    b_f = jnp.pad(b_forget, ((0, 0), (0, LANES - heads_a)))[:, None, :]
    ffa_gu16, ffa_down16 = ffa_gu.astype(BF16), ffa_down.astype(BF16)
    ffb_gu16, ffb_down16 = ffb_gu.astype(BF16), ffb_down.astype(BF16)
    w_up_a16, w_up_b16, w_up_c16 = w_up_a.astype(BF16), w_up_b.astype(BF16), w_up_c.astype(BF16)
    w_out16 = w_out.astype(BF16)
    ln_g3 = ln_g.reshape(depth * 3, 1, d)
    ln_b3 = ln_b.reshape(depth * 3, 1, d)
    vln_g3, vln_b3 = vln_g[:, None, :], vln_b[:, None, :]
    bs_t = jnp.swapaxes(b_spatial, 1, 2)
    ws_row = jnp.repeat(w_spatial[:, :, 0, 0], cg, axis=1)[:, None, :]
    bs_row = jnp.repeat(b_spatial[:, :, 0], cg, axis=1)[:, None, :]
    cache_k4 = cache_k.reshape(depth, n_pool, page, wa)
    cache_v4 = cache_v.reshape(depth, n_pool, page, wa)
    cache_lft = jnp.swapaxes(cache_logf, 2, 3)

    pos = jnp.concatenate([jnp.tile(jnp.arange(seq, dtype=F32), batch), jnp.full((SAMPLE_ROWS,), past_len, F32)])
    cos_t, sin_t = _rope_tables(jnp.broadcast_to(pos[:, None], (mt, dk)), tm=tm)

    x = jnp.concatenate([x_prompt.reshape(n_prompt, d), x_sample.reshape(dec_batch, d),
                         jnp.zeros((SAMPLE_ROWS - dec_batch, d), F32)], axis=0)

    col_b = 3 * wa
    col_c = col_b + 2 * wqk + 2 * wv
    col_g = col_c + 2 * wc
    outs = [[] for _ in range(9)]
    for l in range(depth):
        x = _ffn(x, ffa_gu16, ffa_down16, ln_g3, ln_b3, l, 3 * l, alpha=alpha, tm=tm)

        z_a, lf, xb = _proj_a(x, w_main, w_f, b_f, l, ncols=3 * wa, tm=tm)
        z_b = _proj_b(xb, w_main, cos_t, sin_t, l, col0=col_b, wqk=wqk, wv=wv, dk=dk, tm=tm)
        u_c, vn = _proj_c(xb, w_main, vln_g3, vln_b3, l, col0=col_c, wc=wc, tm=tm)
        gates = _proj_gate(xb, w_main, l, col0=col_g, ncols=3 * d, tm=tm)

        c = _cumsum(lf, batch=batch, seq=seq)
        tq = _pick_tile(seq, 512, LANES)
        ck = jnp.swapaxes(c[:, :heads_a].reshape(batch, seq, heads_a), 1, 2).reshape(batch * heads_a, seq // tq, 1, tq)
        o_a = _fox_prompt(z_a, ck, batch=batch, seq=seq, heads=heads_a, hd=hd)
        o_a_s = _fox_sample(z_a, lf, cache_k4, cache_v4, cache_lft, page_table, l,
                            row0=n_prompt, dec_batch=dec_batch, heads=heads_a, hd=hd)

        y_b, st_p = _ret_prompt(z_b, ret_gn_g, ret_gn_b, l, batch=batch, seq=seq, heads=heads_b, dk=dk, dv=dv)
        y_b_s, y_c_s, st_s = _sample_mix(z_b, state_ret, ret_gn_g, ret_gn_b, u_c, vn, ws_row, bs_row, l,
                                         row0=n_prompt, dec_batch=dec_batch, heads=heads_b, dk=dk, dv=dv)
        y_c = _chunk_mix(vn, u_c, w_spatial, bs_t, l, rows=n_prompt, groups=groups, cg=cg)

        y = _merge(_with_sample_rows(o_a, o_a_s), _with_sample_rows(y_b, y_b_s), _with_sample_rows(y_c, y_c_s),
                   gates, w_up_a16, w_up_b16, w_up_c16, l, tm=tm)
        x = _out_proj(x, y, w_out16, ln_g3, ln_b3, l, 3 * l + 1, alpha=alpha)
        x = _ffn(x, ffb_gu16, ffb_down16, ln_g3, ln_b3, l, 3 * l + 2, alpha=alpha, tm=tm)

        s0, s1 = n_prompt, n_prompt + dec_batch
        outs[0].append(z_a[:n_prompt, wa:2 * wa].reshape(batch, seq, heads_a, hd))
        outs[1].append(z_a[:n_prompt, 2 * wa:].reshape(batch, seq, heads_a, hd))
        outs[2].append(lf[:n_prompt, :heads_a].reshape(batch, seq, heads_a))
        outs[3].append(st_p)
        outs[4].append(z_a[s0:s1, wa:2 * wa].reshape(dec_batch, 1, heads_a, hd))
        outs[5].append(z_a[s0:s1, 2 * wa:].reshape(dec_batch, 1, heads_a, hd))
        outs[6].append(lf[s0:s1, :heads_a].reshape(dec_batch, 1, heads_a))
        outs[7].append(st_s)
        outs[8].append(vn[s0:s1].reshape(dec_batch, 1, groups, cg))

    y_prompt = x[:n_prompt].reshape(batch, seq, d)
    y_sample = x[n_prompt:n_prompt + dec_batch].reshape(dec_batch, 1, d)
    return (y_prompt, y_sample) + tuple(jnp.stack(o) for o in outs)
```

```python
import functools
import math

import jax
import jax.numpy as jnp
from jax import lax
from jax.experimental import pallas as pl
from jax.experimental.pallas import tpu as pltpu

F32 = jnp.float32
BF16 = jnp.bfloat16

LN_EPS = 1e-5
ROPE_BASE = 10000.0
NEG_BIG = -1e30
LANES = 128
SAMPLE_ROWS = 16
V7X_VMEM_LIMIT = 56 * 1024 * 1024


def _pick_tile(n, max_tile, mult):
    best = None
    for t in range(mult, min(n, max_tile) + 1, mult):
        if n % t == 0:
            best = t
    assert best is not None, (n, max_tile, mult)
    return best


def _params(sem):
    return pltpu.CompilerParams(dimension_semantics=sem, vmem_limit_bytes=V7X_VMEM_LIMIT)


def _layer_norm(y, g, b):
    mu = jnp.mean(y, axis=-1, keepdims=True)
    d = y - mu
    var = jnp.mean(d * d, axis=-1, keepdims=True)
    return d * lax.rsqrt(var + LN_EPS) * g + b


def _silu(x):
    return x * jax.nn.sigmoid(x)


def _gelu_tanh(x):
    c = math.sqrt(2.0 / math.pi)
    return 0.5 * x * (1.0 + jnp.tanh(c * (x + 0.044715 * (x * x * x))))


def _log_sigmoid(x):
    return jnp.minimum(x, 0.0) - jnp.log1p(jnp.exp(-jnp.abs(x)))


def _split3(x):
    hi = x.astype(BF16)
    r = x - hi.astype(F32)
    mid = r.astype(BF16)
    lo = (r - mid.astype(F32)).astype(BF16)
    return hi, mid, lo


def _ffn_kernel(x_ref, wg_ref, wu_ref, wd_ref, g_ref, b_ref, o_ref, xb_ref, *, alpha, nf):
    f = pl.program_id(1)

    @pl.when(f == 0)
    def _():
        xb_ref[...] = x_ref[...].astype(BF16)
        o_ref[...] = jnp.zeros_like(o_ref)

    xb = xb_ref[...]
    hg = jnp.dot(xb, wg_ref[...], preferred_element_type=F32)
    hu = jnp.dot(xb, wu_ref[...], preferred_element_type=F32)
    a = (_silu(hg) * hu).astype(BF16)
    o_ref[...] += jnp.dot(a, wd_ref[...], preferred_element_type=F32)

    @pl.when(f == nf - 1)
    def _():
        o_ref[...] = _layer_norm(alpha * x_ref[...] + 0.5 * o_ref[...], g_ref[...], b_ref[...])


def _ffn(x, w_gu, w_down, ln_g, ln_b, l, ln_idx, *, alpha, tm):
    mt, d = x.shape
    dff = w_down.shape[1]
    tf = _pick_tile(dff, 512, LANES)
    nf = dff // tf
    return pl.pallas_call(
        functools.partial(_ffn_kernel, alpha=alpha, nf=nf),
        grid=(mt // tm, nf),
        in_specs=[
            pl.BlockSpec((tm, d), lambda m, f: (m, 0), pipeline_mode=pl.Buffered(1)),
            pl.BlockSpec((None, d, tf), lambda m, f: (l, 0, f)),
            pl.BlockSpec((None, d, tf), lambda m, f: (l, 0, f + nf)),
            pl.BlockSpec((None, tf, d), lambda m, f: (l, f, 0)),
            pl.BlockSpec((None, 1, d), lambda m, f: (ln_idx, 0, 0)),
            pl.BlockSpec((None, 1, d), lambda m, f: (ln_idx, 0, 0)),
        ],
        out_specs=pl.BlockSpec((tm, d), lambda m, f: (m, 0)),
        out_shape=jax.ShapeDtypeStruct((mt, d), F32),
        scratch_shapes=[pltpu.VMEM((tm, d), BF16)],
        compiler_params=_params(("parallel", "arbitrary")),
        name="ffn",
    )(x, w_gu, w_gu, w_down, ln_g, ln_b)


def _proj_a_kernel(x_ref, w_ref, wf_ref, bf_ref, z_ref, lf_ref, xb_ref):
    @pl.when(pl.program_id(1) == 0)
    def _():
        xb0 = x_ref[...].astype(BF16)
        xb_ref[...] = xb0
        fa = jnp.dot(xb0, wf_ref[...], preferred_element_type=F32) + bf_ref[...]
        lf_ref[...] = _log_sigmoid(fa)

    z_ref[...] = jnp.dot(xb_ref[...], w_ref[...], preferred_element_type=F32)


def _proj_a(x, w_main, w_f, b_f, l, *, ncols, tm):
    mt, d = x.shape
    tn = _pick_tile(ncols, 512, LANES)
    return pl.pallas_call(
        _proj_a_kernel,
        grid=(mt // tm, ncols // tn),
        in_specs=[
            pl.BlockSpec((tm, d), lambda m, n: (m, 0)),
            pl.BlockSpec((None, d, tn), lambda m, n: (l, 0, n)),
            pl.BlockSpec((None, d, LANES), lambda m, n: (l, 0, 0)),
            pl.BlockSpec((None, 1, LANES), lambda m, n: (l, 0, 0)),
        ],
        out_specs=[
            pl.BlockSpec((tm, tn), lambda m, n: (m, n)),
            pl.BlockSpec((tm, LANES), lambda m, n: (m, 0)),
            pl.BlockSpec((tm, d), lambda m, n: (m, 0)),
        ],
        out_shape=[jax.ShapeDtypeStruct((mt, ncols), F32), jax.ShapeDtypeStruct((mt, LANES), F32),
                   jax.ShapeDtypeStruct((mt, d), BF16)],
        compiler_params=_params(("parallel", "arbitrary")),
        name="proj_a",
    )(x, w_main, w_f, b_f)


def _proj_b_kernel(xb_ref, w_ref, cos_ref, sin_ref, z_ref, *, n_qk, n_v, dk, kscale):
    n = pl.program_id(1)
    acc = jnp.dot(xb_ref[...], w_ref[...], preferred_element_type=F32)
    tn = acc.shape[1]

    @pl.when(n < 2 * n_qk)
    def _():
        cos = cos_ref[...]
        sin = sin_ref[...]
        scale = jnp.where(n >= n_qk, kscale, 1.0).astype(F32)
        for h in range(tn // dk):
            xh = acc[:, h * dk:(h + 1) * dk]
            r = xh * cos + pltpu.roll(xh, dk // 2, 1) * sin
            z_ref[:, h * dk:(h + 1) * dk] = (r * scale).astype(BF16)

    @pl.when(jnp.logical_and(n >= 2 * n_qk, n < 2 * n_qk + n_v))
    def _():
        z_ref[...] = acc.astype(BF16)

    @pl.when(n >= 2 * n_qk + n_v)
    def _():
        z_ref[...] = _silu(acc).astype(BF16)


def _proj_b(xb, w_main, cos_t, sin_t, l, *, col0, wqk, wv, dk, tm):
    mt, d = xb.shape
    assert dk == LANES
    tn = _pick_tile(wqk, 512, dk)
    assert wv % tn == 0 and col0 % tn == 0
    ncols = 2 * wqk + 2 * wv
    kern = functools.partial(_proj_b_kernel, n_qk=wqk // tn, n_v=wv // tn, dk=dk, kscale=dk ** -0.5)
    return pl.pallas_call(
        kern,
        grid=(mt // tm, ncols // tn),
        in_specs=[
            pl.BlockSpec((tm, d), lambda m, n: (m, 0)),
            pl.BlockSpec((None, d, tn), lambda m, n: (l, 0, n + col0 // tn)),
            pl.BlockSpec((tm, dk), lambda m, n: (m, 0)),
            pl.BlockSpec((tm, dk), lambda m, n: (m, 0)),
        ],
        out_specs=pl.BlockSpec((tm, tn), lambda m, n: (m, n)),
        out_shape=jax.ShapeDtypeStruct((mt, ncols), BF16),
        compiler_params=_params(("parallel", "arbitrary")),
        name="proj_b",
    )(xb, w_main, cos_t, sin_t)


def _proj_c_kernel(xb_ref, w_ref, g_ref, b_ref, u_ref, vn_ref):
    n = pl.program_id(1)
    acc = _gelu_tanh(jnp.dot(xb_ref[...], w_ref[...], preferred_element_type=F32))

    @pl.when(n == 0)
    def _():
        u_ref[...] = acc.astype(BF16)

    @pl.when(n == 1)
    def _():
        vn_ref[...] = _layer_norm(acc, g_ref[...], b_ref[...])


def _proj_c(xb, w_main, vln_g, vln_b, l, *, col0, wc, tm):
    mt, d = xb.shape
    assert col0 % wc == 0
    return pl.pallas_call(
        _proj_c_kernel,
        grid=(mt // tm, 2),
        in_specs=[
            pl.BlockSpec((tm, d), lambda m, n: (m, 0)),
            pl.BlockSpec((None, d, wc), lambda m, n: (l, 0, n + col0 // wc)),
            pl.BlockSpec((None, 1, wc), lambda m, n: (l, 0, 0)),
            pl.BlockSpec((None, 1, wc), lambda m, n: (l, 0, 0)),
        ],
        out_specs=[
            pl.BlockSpec((tm, wc), lambda m, n: (m, 0)),
            pl.BlockSpec((tm, wc), lambda m, n: (m, 0)),
        ],
        out_shape=[jax.ShapeDtypeStruct((mt, wc), BF16), jax.ShapeDtypeStruct((mt, wc), F32)],
        compiler_params=_params(("parallel", "arbitrary")),
        name="proj_c",
    )(xb, w_main, vln_g, vln_b)


def _proj_gate_kernel(xb_ref, w_ref, z_ref):
    acc = jnp.dot(xb_ref[...], w_ref[...], preferred_element_type=F32)
    z_ref[...] = jax.nn.sigmoid(acc).astype(BF16)


def _proj_gate(xb, w_main, l, *, col0, ncols, tm):
    mt, d = xb.shape
    tn = _pick_tile(math.gcd(ncols, col0), 512, LANES)
    return pl.pallas_call(
        _proj_gate_kernel,
        grid=(mt // tm, ncols // tn),
        in_specs=[
            pl.BlockSpec((tm, d), lambda m, n: (m, 0)),
            pl.BlockSpec((None, d, tn), lambda m, n: (l, 0, n + col0 // tn)),
        ],
        out_specs=pl.BlockSpec((tm, tn), lambda m, n: (m, n)),
        out_shape=jax.ShapeDtypeStruct((mt, ncols), BF16),
        compiler_params=_params(("parallel", "arbitrary")),
        name="proj_gate",
    )(xb, w_main)


def _rope_kernel(pos_ref, cos_ref, sin_ref, *, half):
    lane = lax.broadcasted_iota(jnp.int32, pos_ref.shape, 1)
    j = jnp.where(lane >= half, lane - half, lane).astype(F32)
    inv = jnp.exp(j * (-math.log(ROPE_BASE) / half))
    ang = pos_ref[...] * inv
    cos_ref[...] = jnp.cos(ang)
    sin_ref[...] = jnp.where(lane >= half, 1.0, -1.0) * jnp.sin(ang)


def _rope_tables(pos_b, *, tm):
    mt, dk = pos_b.shape
    spec = pl.BlockSpec((tm, dk), lambda m: (m, 0))
    return pl.pallas_call(
        functools.partial(_rope_kernel, half=dk // 2),
        grid=(mt // tm,),
        in_specs=[spec],
        out_specs=[spec, spec],
        out_shape=[jax.ShapeDtypeStruct((mt, dk), F32)] * 2,
        compiler_params=_params(("parallel",)),
        name="rope_tables",
    )(pos_b)


def _cumsum_kernel(lf_ref, c_ref, carry_ref):
    @pl.when(pl.program_id(1) == 0)
    def _():
        carry_ref[...] = jnp.zeros_like(carry_ref)

    x = lf_ref[...]
    tc = x.shape[0]
    row = lax.broadcasted_iota(jnp.int32, (tc, tc), 0)
    col = lax.broadcasted_iota(jnp.int32, (tc, tc), 1)
    tri = jnp.where(col <= row, 1.0, 0.0).astype(BF16)
    c = carry_ref[0:1, :]
    for part in _split3(x):
        c = c + jnp.dot(tri, part, preferred_element_type=F32)
    c_ref[...] = c
    carry_ref[...] = jnp.broadcast_to(c[tc - 1:tc, :], carry_ref.shape)


def _cumsum(lf, *, batch, seq):
    tc = _pick_tile(seq, 512, 8)
    nt = seq // tc
    return pl.pallas_call(
        _cumsum_kernel,
        grid=(batch, nt),
        in_specs=[pl.BlockSpec((tc, LANES), lambda b, t: (b * nt + t, 0))],
        out_specs=pl.BlockSpec((tc, LANES), lambda b, t: (b * nt + t, 0)),
        out_shape=jax.ShapeDtypeStruct((batch * seq, LANES), F32),
        scratch_shapes=[pltpu.VMEM((8, LANES), F32)],
        compiler_params=_params(("parallel", "arbitrary")),
        name="logf_cumsum",
    )(lf)


def _fox_prompt_kernel(base_ref, q_ref, k_ref, v_ref, ck_ref, o_ref, kb_ref, vb_ref, m_ref, l_ref, acc_ref,
                       *, tq, hps, hd, scale):
    del base_ref
    i = pl.program_id(2)
    log2e = math.log2(math.e)

    @pl.when(i == 0)
    def _():
        kb_ref[...] = k_ref[...].astype(BF16)
        vb_ref[...] = v_ref[...].astype(BF16)

    qs = [(q_ref[:, hh * hd:(hh + 1) * hd] * (scale * log2e)).astype(BF16) for hh in range(hps)]
    m_ref[...] = jnp.full_like(m_ref, NEG_BIG)
    l_ref[...] = jnp.zeros_like(l_ref)
    acc_ref[...] = jnp.zeros_like(acc_ref)

    def step(j, masked):
        off = pl.multiple_of(j * tq, tq)
        for hh in range(hps):
            k = kb_ref[pl.ds(off, tq), hh * hd:(hh + 1) * hd]
            v = vb_ref[pl.ds(off, tq), hh * hd:(hh + 1) * hd]
            s = lax.dot_general(qs[hh], k, (((1,), (1,)), ((), ())), preferred_element_type=F32)
            s = s - ck_ref[hh, j] * log2e
            if masked:
                row = lax.broadcasted_iota(jnp.int32, s.shape, 0)
                col = lax.broadcasted_iota(jnp.int32, s.shape, 1)
                s = jnp.where(col <= row, s, NEG_BIG)
            m_prev = m_ref[hh]
            m_new = jnp.maximum(m_prev, jnp.max(s, axis=1, keepdims=True))
            p = jnp.exp2(s - jnp.tile(m_new, (1, tq // LANES)))
            alpha = jnp.exp2(m_prev - m_new)
            l_ref[hh] = alpha * l_ref[hh] + jnp.sum(p, axis=1, keepdims=True)
            acc_ref[hh] = alpha * acc_ref[hh] + jnp.dot(p.astype(BF16), v, preferred_element_type=F32)
            m_ref[hh] = m_new

    def body(j, carry):
        step(j, False)
        return carry

    lax.fori_loop(0, i, body, 0)
    step(i, True)
    for hh in range(hps):
        o_ref[:, hh * hd:(hh + 1) * hd] = (acc_ref[hh] / l_ref[hh]).astype(BF16)


def _fox_prompt(base, z_a, ck, *, batch, seq, heads, hd):
    assert hd == LANES
    tq = _pick_tile(seq, 512, LANES)
    nq = seq // tq
    hps = 2 if heads % 2 == 0 else 1
    wb = hps * hd
    ng = heads // hps
    kern = functools.partial(_fox_prompt_kernel, tq=tq, hps=hps, hd=hd, scale=hd ** -0.5)
    return pl.pallas_call(
        kern,
        grid=(batch, ng, nq),
        in_specs=[
            pl.BlockSpec(memory_space=pl.ANY),
            pl.BlockSpec((tq, wb), lambda b, h, i: (b * nq + i, h)),
            pl.BlockSpec((seq, wb), lambda b, h, i: (b, ng + h)),
            pl.BlockSpec((seq, wb), lambda b, h, i: (b, 2 * ng + h)),
            pl.BlockSpec((hps, nq, 1, tq), lambda b, h, i: (b * ng + h, 0, 0, 0)),
        ],
        out_specs=pl.BlockSpec((tq, wb), lambda b, h, i: (b * nq + i, h)),
        out_shape=jax.ShapeDtypeStruct(base.shape, base.dtype),
        input_output_aliases={0: 0},
        scratch_shapes=[
            pltpu.VMEM((seq, wb), BF16),
            pltpu.VMEM((seq, wb), BF16),
            pltpu.VMEM((hps, tq, LANES), F32),
            pltpu.VMEM((hps, tq, LANES), F32),
            pltpu.VMEM((hps, tq, hd), F32),
        ],
        compiler_params=_params(("parallel", "parallel", "arbitrary")),
        name="fox_prompt",
    )(base, z_a, z_a, z_a, ck)


def _fox_sample_kernel(pt_ref, qkv_ref, lfn_ref, ck_hbm, cv_hbm, lf_hbm, o_ref,
                       kbuf, vbuf, lbuf, sem, qs_ref, m_ref, l_ref, acc_ref, carry_ref,
                       *, layer, heads, hd, scale, n_pages, gp, dec_batch):
    b = pl.program_id(0)
    g = pl.program_id(1)
    ng = n_pages // gp
    step = b * ng + g
    slot = step % 2
    rows = lbuf.shape[2]

    def copies(bb, gg, sl):
        out = []
        for p in range(gp):
            pid = pt_ref[bb, n_pages - 1 - (gg * gp + p)]
            out.append(pltpu.make_async_copy(ck_hbm.at[layer, pid], kbuf.at[sl, p], sem.at[0, sl]))
            out.append(pltpu.make_async_copy(cv_hbm.at[layer, pid], vbuf.at[sl, p], sem.at[1, sl]))
            out.append(pltpu.make_async_copy(lf_hbm.at[layer, pid], lbuf.at[sl, p], sem.at[2, sl]))
        return out

    @pl.when(step == 0)
    def _():
        for c in copies(b, g, slot):
            c.start()

    @pl.when(step + 1 < dec_batch * ng)
    def _():
        wrap = g + 1 == ng
        for c in copies(jnp.where(wrap, b + 1, b), jnp.where(wrap, 0, g + 1), 1 - slot):
            c.start()

    lane = lax.broadcasted_iota(jnp.int32, (LANES, LANES), 1)
    src = lax.broadcasted_iota(jnp.int32, (LANES, LANES), 0)
    same_head = (src % heads) == (lane % heads)

    @pl.when(g == 0)
    def _():
        q = qkv_ref[0:heads, :] * scale
        qs_ref[...] = q.astype(BF16)
        m_ref[...] = jnp.sum(q * qkv_ref[heads:2 * heads, :], axis=1, keepdims=True)
        l_ref[...] = jnp.ones_like(l_ref)
        acc_ref[...] = qkv_ref[2 * heads:3 * heads, :]
        spread = jnp.where(src == lane % heads, 1.0, 0.0).astype(BF16)
        carry = jnp.zeros((1, LANES), F32)
        for part in _split3(lfn_ref[...]):
            carry = carry + jnp.dot(part, spread, preferred_element_type=F32)
        carry_ref[...] = carry

    for c in copies(b, g, slot):
        c.wait()

    later_w = jnp.where(jnp.logical_and(same_head, src > lane), 1.0, 0.0).astype(BF16)
    whole_w = jnp.where(same_head, 1.0, 0.0).astype(BF16)
    mix_w = jnp.concatenate([later_w, whole_w], axis=1)
    lf_all = lbuf[slot].reshape(gp * rows, LANES)
    sums = jnp.zeros((gp * rows, 2 * LANES), F32)
    for part in _split3(lf_all):
        sums = sums + jnp.dot(part, mix_w, preferred_element_type=F32)

    row_id = lax.broadcasted_iota(jnp.int32, (rows, LANES), 0)
    valid = (lax.broadcasted_iota(jnp.int32, (heads, LANES), 1) % heads
             == lax.broadcasted_iota(jnp.int32, (heads, LANES), 0))
    qs = qs_ref[...]
    carry = carry_ref[...]
    tiles = []
    for p in range(gp):
        within = sums[p * rows:(p + 1) * rows, :LANES]
        tot = sums[p * rows:(p + 1) * rows, LANES:]
        suf = tot
        sh = 1
        while sh < rows:
            suf = suf + jnp.where(row_id < rows - sh, pltpu.roll(suf, rows - sh, 0), 0.0)
            sh *= 2
        bias = carry + within + (suf - tot)
        carry = carry + suf[0:1, :]
        k2 = kbuf[slot, p].reshape(rows * LANES, hd).astype(BF16)
        s = lax.dot_general(qs, k2, (((1,), (1,)), ((), ())), preferred_element_type=F32)
        for r in range(rows):
            t = s[:, r * LANES:(r + 1) * LANES] + jnp.broadcast_to(bias[r:r + 1, :], (heads, LANES))
            tiles.append(jnp.where(valid, t, NEG_BIG))
    carry_ref[...] = carry

    m_tile = tiles[0]
    for t in tiles[1:]:
        m_tile = jnp.maximum(m_tile, t)
    m_prev = m_ref[...]
    m_new = jnp.maximum(m_prev, jnp.max(m_tile, axis=1, keepdims=True))
    alpha = jnp.exp(m_prev - m_new)
    probs = [jnp.exp(t - m_new) for t in tiles]
    p_sum = probs[0]
    for t in probs[1:]:
        p_sum = p_sum + t
    l_ref[...] = alpha * l_ref[...] + jnp.sum(p_sum, axis=1, keepdims=True)
    acc = alpha * acc_ref[...]
    for p in range(gp):
        pp = jnp.concatenate(probs[p * rows:(p + 1) * rows], axis=1).astype(BF16)
        v2 = vbuf[slot, p].reshape(rows * LANES, hd).astype(BF16)
        acc = acc + jnp.dot(pp, v2, preferred_element_type=F32)
    acc_ref[...] = acc
    m_ref[...] = m_new

    @pl.when(g == ng - 1)
    def _():
        o_ref[...] = acc_ref[...] / l_ref[...]


def _fox_sample(qkv_s, lf_s, cache_k, cache_v, cache_lf, page_table, l, *, heads, hd):
    dec_batch = qkv_s.shape[0]
    n_pages = page_table.shape[1]
    page = cache_k.shape[2]
    rows = cache_lf.shape[2]
    assert LANES % heads == 0 and rows * LANES == page * heads
    gp = _pick_tile(n_pages, 8, 1)
    kern = functools.partial(_fox_sample_kernel, layer=l, heads=heads, hd=hd, scale=hd ** -0.5,
                             n_pages=n_pages, gp=gp, dec_batch=dec_batch)
    grid_spec = pltpu.PrefetchScalarGridSpec(
        num_scalar_prefetch=1,
        grid=(dec_batch, n_pages // gp),
        in_specs=[
            pl.BlockSpec((None, 3 * heads, hd), lambda b, g, pt: (b, 0, 0)),
            pl.BlockSpec((None, 1, LANES), lambda b, g, pt: (b, 0, 0)),
            pl.BlockSpec(memory_space=pl.ANY),
            pl.BlockSpec(memory_space=pl.ANY),
            pl.BlockSpec(memory_space=pl.ANY),
        ],
        out_specs=pl.BlockSpec((None, heads, hd), lambda b, g, pt: (b, 0, 0)),
        scratch_shapes=[
            pltpu.VMEM((2, gp, page, heads, hd), F32),
            pltpu.VMEM((2, gp, page, heads, hd), F32),
            pltpu.VMEM((2, gp, rows, LANES), F32),
            pltpu.SemaphoreType.DMA((3, 2)),
            pltpu.VMEM((heads, hd), BF16),
            pltpu.VMEM((heads, 1), F32),
            pltpu.VMEM((heads, 1), F32),
            pltpu.VMEM((heads, hd), F32),
            pltpu.VMEM((1, LANES), F32),
        ],
    )
    return pl.pallas_call(
        kern,
        grid_spec=grid_spec,
        out_shape=jax.ShapeDtypeStruct((dec_batch, heads, hd), F32),
        compiler_params=_params(("arbitrary", "arbitrary")),
        name="fox_sample",
    )(page_table, qkv_s, lf_s, cache_k, cache_v, cache_lf)


def _log_gammas(n_heads):
    return [math.log1p(-(2.0 ** (-5.0 - h))) for h in range(n_heads)]


def _group_norm_gate(o, gate, g, b):
    mu = jnp.mean(o, axis=-1, keepdims=True)
    d = o - mu
    var = jnp.mean(d * d, axis=-1, keepdims=True)
    return gate * (d * lax.rsqrt(var + LN_EPS) * g + b)


def _ret_prompt_kernel(base_ref, q_ref, k_ref, v_ref, gate_ref, g_ref, b_ref, y_ref, st_ref,
                       *, heads, dk, dv, chunk):
    del base_ref

    @pl.when(pl.program_id(1) == 0)
    def _():
        st_ref[...] = jnp.zeros_like(st_ref)

    n_col = lax.broadcasted_iota(jnp.int32, (chunk, 1), 0).astype(F32)
    diff = (lax.broadcasted_iota(jnp.int32, (chunk, chunk), 0)
            - lax.broadcasted_iota(jnp.int32, (chunk, chunk), 1)).astype(F32)
    for h, lg in enumerate(_log_gammas(heads)):
        q = q_ref[:, h * dk:(h + 1) * dk]
        k = k_ref[:, h * dk:(h + 1) * dk]
        v = v_ref[:, h * dv:(h + 1) * dv]
        state = st_ref[h]
        dmask = jnp.where(diff >= 0, jnp.exp(lg * jnp.maximum(diff, 0.0)), 0.0)
        a = lax.dot_general(q, k, (((1,), (1,)), ((), ())), preferred_element_type=F32) * dmask
        o = jnp.dot(a.astype(BF16), v, preferred_element_type=F32)
        o = o + jnp.dot(q, state.astype(BF16), preferred_element_type=F32) * jnp.exp(lg * (n_col + 1.0))
        kdec = k.astype(F32) * jnp.exp(lg * (chunk - 1.0 - n_col))
        st_ref[h] = state * math.exp(lg * chunk) + jnp.dot(kdec.T.astype(BF16), v, preferred_element_type=F32)
        gate = gate_ref[:, h * dv:(h + 1) * dv].astype(F32)
        y = _group_norm_gate(o, gate, g_ref[h:h + 1, :], b_ref[h:h + 1, :])
        y_ref[:, h * dv:(h + 1) * dv] = y.astype(BF16)


def _ret_prompt(base, z_b, gn_g, gn_b, l, *, batch, seq, heads, dk, dv):
    chunk = LANES
    nc = seq // chunk
    wqk, wv = heads * dk, heads * dv
    assert wv % wqk == 0
    kern = functools.partial(_ret_prompt_kernel, heads=heads, dk=dk, dv=dv, chunk=chunk)
    return pl.pallas_call(
        kern,
        grid=(batch, nc),
        in_specs=[
            pl.BlockSpec(memory_space=pl.ANY),
            pl.BlockSpec((chunk, wqk), lambda b, c: (b * nc + c, 0)),
            pl.BlockSpec((chunk, wqk), lambda b, c: (b * nc + c, 1)),
            pl.BlockSpec((chunk, wv), lambda b, c: (b * nc + c, 2 * wqk // wv)),
            pl.BlockSpec((chunk, wv), lambda b, c: (b * nc + c, 2 * wqk // wv + 1)),
            pl.BlockSpec((None, heads, dv), lambda b, c: (l, 0, 0)),
            pl.BlockSpec((None, heads, dv), lambda b, c: (l, 0, 0)),
        ],
        out_specs=[
            pl.BlockSpec((chunk, wv), lambda b, c: (b * nc + c, 0)),
            pl.BlockSpec((None, heads, dk, dv), lambda b, c: (b, 0, 0, 0)),
        ],
        out_shape=[
            jax.ShapeDtypeStruct(base.shape, base.dtype),
            jax.ShapeDtypeStruct((batch, heads, dk, dv), F32),
        ],
        input_output_aliases={0: 0},
        compiler_params=_params(("parallel", "arbitrary")),
        name="ret_prompt",
    )(base, z_b, z_b, z_b, z_b, gn_g, gn_b)


def _sample_mix_kernel(zb_ref, st_ref, g_ref, b_ref, u_ref, vn_ref, ws_ref, bs_ref, y_ref, yc_ref, nst_ref,
                       *, heads, dk, dv):
    b = pl.program_id(0)

    @pl.when(b == 0)
    def _():
        y_ref[...] = jnp.zeros_like(y_ref)
        yc_ref[...] = u_ref[...].astype(F32) * (ws_ref[...] * vn_ref[...] + bs_ref[...])

    wqk, wv = heads * dk, heads * dv
    blk = zb_ref[...].astype(F32)
    pick = lax.broadcasted_iota(jnp.int32, blk.shape, 0) == b
    row = jnp.sum(jnp.where(pick, blk, 0.0), axis=0, keepdims=True)
    eye =(lax.broadcasted_iota(jnp.int32, (dk, dk), 0) == lax.broadcasted_iota(jnp.int32, (dk, dk), 1))
    for h, lg in enumerate(_log_gammas(heads)):
        q = row[:, h * dk:(h + 1) * dk]
        k = row[:, wqk + h * dk:wqk + (h + 1) * dk]
        v = row[:, 2 * wqk + h * dv:2 * wqk + (h + 1) * dv]
        gate = row[:, 2 * wqk + wv + h * dv:2 * wqk + wv + (h + 1) * dv]
        qcol = jnp.sum(jnp.where(eye, jnp.broadcast_to(q, (dk, dk)), 0.0), axis=1, keepdims=True)
        kcol = jnp.sum(jnp.where(eye, jnp.broadcast_to(k, (dk, dk)), 0.0), axis=1, keepdims=True)
        state = st_ref[h]
        gamma = math.exp(lg)
        new_state = state * gamma + kcol * v
        nst_ref[h] = new_state
        o = jnp.sum(q * k, axis=1, keepdims=True) * v + jnp.sum(qcol * state, axis=0, keepdims=True) * gamma
        y = _group_norm_gate(o, gate, g_ref[h:h + 1, :], b_ref[h:h + 1, :])
        y_ref[pl.ds(b, 1), h * dv:(h + 1) * dv] = y


def _sample_mix(z_b, state_ret, gn_g, gn_b, u_c, vn, ws_row, bs_row, l, *, row0, dec_batch, heads, dk, dv):
    wv = heads * dv
    wc = u_c.shape[1]
    rb = row0 // SAMPLE_ROWS
    kern = functools.partial(_sample_mix_kernel, heads=heads, dk=dk, dv=dv)
    return pl.pallas_call(
        kern,
        grid=(dec_batch,),
        in_specs=[
            pl.BlockSpec((SAMPLE_ROWS, z_b.shape[1]), lambda b: (rb, 0)),
            pl.BlockSpec((None, None, heads, dk, dv), lambda b: (l, b, 0, 0, 0)),
            pl.BlockSpec((None, heads, dv), lambda b: (l, 0, 0)),
            pl.BlockSpec((None, heads, dv), lambda b: (l, 0, 0)),
            pl.BlockSpec((SAMPLE_ROWS, wc), lambda b: (rb, 0)),
            pl.BlockSpec((SAMPLE_ROWS, wc), lambda b: (rb, 0)),
            pl.BlockSpec((None, 1, wc), lambda b: (l, 0, 0)),
            pl.BlockSpec((None, 1, wc), lambda b: (l, 0, 0)),
        ],
        out_specs=[
            pl.BlockSpec((SAMPLE_ROWS, wv), lambda b: (0, 0)),
            pl.BlockSpec((SAMPLE_ROWS, wc), lambda b: (0, 0)),
            pl.BlockSpec((None, heads, dk, dv), lambda b: (b, 0, 0, 0)),
        ],
        out_shape=[
            jax.ShapeDtypeStruct((SAMPLE_ROWS, wv), F32),
            jax.ShapeDtypeStruct((SAMPLE_ROWS, wc), F32),
            jax.ShapeDtypeStruct((dec_batch, heads, dk, dv), F32),
        ],
        compiler_params=_params(("arbitrary",)),
        name="sample_mix",
    )(z_b, state_ret, gn_g, gn_b, u_c, vn, ws_row, bs_row)


def _chunk_mix_kernel(base_ref, vn_ref, u_ref, ws_ref, bst_ref, y_ref, *, groups, cg):
    del base_ref
    chunk = vn_ref.shape[0]
    lower = (lax.broadcasted_iota(jnp.int32, (chunk, chunk), 1)
             <= lax.broadcasted_iota(jnp.int32, (chunk, chunk), 0))
    bst = bst_ref[...]
    for g in range(groups):
        w = jnp.where(lower, ws_ref[g], 0.0).astype(BF16)
        vg = vn_ref[:, g * cg:(g + 1) * cg].astype(BF16)
        s = jnp.dot(w, vg, preferred_element_type=F32) + bst[:, g:g + 1]
        y_ref[:, g * cg:(g + 1) * cg] = (u_ref[:, g * cg:(g + 1) * cg].astype(F32) * s).astype(BF16)


def _chunk_mix(base, vn, u_c, w_spatial, bs_t, l, *, rows, groups, cg):
    chunk = w_spatial.shape[2]
    wc = groups * cg
    kern = functools.partial(_chunk_mix_kernel, groups=groups, cg=cg)
    return pl.pallas_call(
        kern,
        grid=(rows // chunk,),
        in_specs=[
            pl.BlockSpec(memory_space=pl.ANY),
            pl.BlockSpec((chunk, wc), lambda n: (n, 0)),
            pl.BlockSpec((chunk, wc), lambda n: (n, 0)),
            pl.BlockSpec((None, groups, chunk, chunk), lambda n: (l, 0, 0, 0)),
            pl.BlockSpec((None, chunk, groups), lambda n: (l, 0, 0)),
        ],
        out_specs=pl.BlockSpec((chunk, wc), lambda n: (n, 0)),
        out_shape=jax.ShapeDtypeStruct(base.shape, base.dtype),
        input_output_aliases={0: 0},
        compiler_params=_params(("parallel",)),
        name="chunk_mix",
    )(base, vn, u_c, w_spatial, bs_t)


def _merge_kernel(oa_ref, yb_ref, yc_ref, ga_ref, gb_ref, gc_ref, wa_ref, wb_ref, wc_ref, y_ref):
    ya = jnp.dot(oa_ref[...], wa_ref[...], preferred_element_type=F32)
    yb = jnp.dot(yb_ref[...], wb_ref[...], preferred_element_type=F32)
    yc = jnp.dot(yc_ref[...], wc_ref[...], preferred_element_type=F32)
    y = ga_ref[...].astype(F32) * ya + gb_ref[...].astype(F32) * yb + gc_ref[...].astype(F32) * yc
    y_ref[...] = y.astype(BF16)


def _merge(o_a, y_b, y_c, gates, w_up_a, w_up_b, w_up_c, l, *, tm):
    mt = o_a.shape[0]
    d = w_up_a.shape[2]
    tn = _pick_tile(d, 512, LANES)
    nn = d // tn

    def act(a):
        return pl.BlockSpec((tm, a.shape[1]), lambda m, n: (m, 0))

    def wgt(w):
        return pl.BlockSpec((None, w.shape[1], tn), lambda m, n: (l, 0, n))

    return pl.pallas_call(
        _merge_kernel,
        grid=(mt // tm, nn),
        in_specs=[
            act(o_a), act(y_b), act(y_c),
            pl.BlockSpec((tm, tn), lambda m, n: (m, n)),
            pl.BlockSpec((tm, tn), lambda m, n: (m, n + nn)),
            pl.BlockSpec((tm, tn), lambda m, n: (m, n + 2 * nn)),
            wgt(w_up_a), wgt(w_up_b), wgt(w_up_c),
        ],
        out_specs=pl.BlockSpec((tm, tn), lambda m, n: (m, n)),
        out_shape=jax.ShapeDtypeStruct((mt, d), BF16),
        compiler_params=_params(("parallel", "arbitrary")),
        name="merge",
    )(o_a, y_b, y_c, gates, gates, gates, w_up_a, w_up_b, w_up_c)


def _out_kernel(x_ref, y_ref, w_ref, g_ref, b_ref, o_ref, *, alpha):
    sub = jnp.dot(y_ref[...], w_ref[...], preferred_element_type=F32)
    o_ref[...] = _layer_norm(alpha * x_ref[...] + sub, g_ref[...], b_ref[...])


def _out_proj(x, y, w_out, ln_g, ln_b, l, ln_idx, *, alpha):
    mt, d = x.shape
    tm = _pick_tile(mt, 512, SAMPLE_ROWS)
    row = pl.BlockSpec((tm, d), lambda m: (m, 0))
    return pl.pallas_call(
        functools.partial(_out_kernel, alpha=alpha),
        grid=(mt // tm,),
        in_specs=[
            row, row,
            pl.BlockSpec((None, d, d), lambda m: (l, 0, 0)),
            pl.BlockSpec((None, 1, d), lambda m: (ln_idx, 0, 0)),
            pl.BlockSpec((None, 1, d), lambda m: (ln_idx, 0, 0)),
        ],
        out_specs=row,
        out_shape=jax.ShapeDtypeStruct((mt, d), F32),
        compiler_params=_params(("parallel",)),
        name="out_proj",
    )(x, y, w_out, ln_g, ln_b)


def _sample_rows_base(sample_rows, n_prompt):
    pad = SAMPLE_ROWS - sample_rows.shape[0]
    return jnp.pad(sample_rows.astype(BF16), ((n_prompt, pad), (0, 0)))


def kernel(x_prompt, x_sample, cache_k, cache_v, cache_logf, state_ret, page_table, w_in, b_forget, w_up_a, w_up_b, w_up_c, w_out, ret_gn_g, ret_gn_b, vln_g, vln_b, w_spatial, b_spatial, ffa_gu, ffa_down, ffb_gu, ffb_down, ln_g, ln_b):
    batch, seq, d = x_prompt.shape
    dec_batch, dec_seq, _ = x_sample.shape
    depth = w_in.shape[0]
    heads_a, hd = cache_k.shape[3], cache_k.shape[4]
    n_pool, page = cache_k.shape[1], cache_k.shape[2]
    heads_b, dk, dv = state_ret.shape[2], state_ret.shape[3], state_ret.shape[4]
    groups, chunk = w_spatial.shape[1], w_spatial.shape[2]
    wa, wqk, wv, wc = heads_a * hd, heads_b * dk, heads_b * dv, vln_g.shape[1]
    cg = wc // groups
    assert dec_seq == 1 and dec_batch <= SAMPLE_ROWS and heads_a <= LANES
    assert seq % chunk == 0 and chunk == LANES and (batch * seq) % SAMPLE_ROWS == 0
    assert w_in.shape[2] == 3 * wa + heads_a + 2 * wqk + 2 * wv + 2 * wc + 3 * d
    past_len = page_table.shape[1] * page
    alpha = (2 * depth) ** 0.25

    n_prompt = batch * seq
    mt = n_prompt + SAMPLE_ROWS
    tm = _pick_tile(mt, 1024, SAMPLE_ROWS)

    w_abc = w_in[:, :, :3 * wa].astype(BF16)
    w_rest = w_in[:, :, 3 * wa + heads_a:].astype(BF16)
    w_f =jnp.pad(w_in[:, :, 3 * wa:3 * wa + heads_a], ((0, 0), (0, 0), (0, LANES - heads_a))).astype(BF16)
    b_f = jnp.pad(b_forget, ((0, 0), (0, LANES - heads_a)))[:, None, :]
    ffa_gu16, ffa_down16 = ffa_gu.astype(BF16), ffa_down.astype(BF16)
    ffb_gu16, ffb_down16 = ffb_gu.astype(BF16), ffb_down.astype(BF16)
    w_up_a16, w_up_b16, w_up_c16 = w_up_a.astype(BF16), w_up_b.astype(BF16), w_up_c.astype(BF16)
    w_out16 = w_out.astype(BF16)
    ln_g3 = ln_g.reshape(depth * 3, 1, d)
    ln_b3 = ln_b.reshape(depth * 3, 1, d)
    vln_g3, vln_b3 = vln_g[:, None, :], vln_b[:, None, :]
    bs_t = jnp.swapaxes(b_spatial, 1, 2)
    ws_row = jnp.repeat(w_spatial[:, :, 0, 0], cg, axis=1)[:, None, :]
    bs_row = jnp.repeat(b_spatial[:, :, 0], cg, axis=1)[:, None, :]
    cache_lf = cache_logf.reshape(depth, n_pool, page * heads_a // LANES, LANES)

    pos = jnp.concatenate([jnp.tile(jnp.arange(seq, dtype=F32), batch), jnp.full((SAMPLE_ROWS,), past_len, F32)])
    cos_t, sin_t = _rope_tables(jnp.broadcast_to(pos[:, None], (mt, dk)), tm=tm)

    x = jnp.concatenate([x_prompt.reshape(n_prompt, d), x_sample.reshape(dec_batch, d),
                         jnp.zeros((SAMPLE_ROWS - dec_batch, d), F32)], axis=0)

    col_c = 2 * wqk + 2 * wv
    col_g = col_c + 2 * wc
    s0, s1 = n_prompt, n_prompt + dec_batch
    outs = [[] for _ in range(9)]
    for l in range(depth):
        x = _ffn(x, ffa_gu16, ffa_down16, ln_g3, ln_b3, l, 3 * l, alpha=alpha, tm=tm)

        z_a, lf, xb = _proj_a(x, w_abc, w_f, b_f, l, ncols=3 * wa, tm=tm)
        z_b = _proj_b(xb, w_rest, cos_t, sin_t, l, col0=0, wqk=wqk, wv=wv, dk=dk, tm=tm)
        u_c, vn = _proj_c(xb, w_rest, vln_g3, vln_b3, l, col0=col_c, wc=wc, tm=tm)
        gates = _proj_gate(xb, w_rest, l, col0=col_g, ncols=3 * d, tm=tm)

        c = _cumsum(lf, batch=batch, seq=seq)
        tq = _pick_tile(seq, 512, LANES)
        ck = jnp.swapaxes(c[:, :heads_a].reshape(batch, seq, heads_a), 1, 2).reshape(batch * heads_a, seq // tq, 1, tq)
        o_a_s = _fox_sample(z_a[s0:s1].reshape(dec_batch, 3 * heads_a, hd), lf[s0:s1, None, :],
                            cache_k, cache_v, cache_lf, page_table, l, heads=heads_a, hd=hd)
        y_b_s, y_c_s, st_s = _sample_mix(z_b, state_ret, ret_gn_g, ret_gn_b, u_c, vn, ws_row, bs_row, l,
                                         row0=n_prompt, dec_batch=dec_batch, heads=heads_b, dk=dk, dv=dv)

        o_a = _fox_prompt(_sample_rows_base(o_a_s.reshape(dec_batch, wa), n_prompt), z_a, ck,
                          batch=batch, seq=seq, heads=heads_a, hd=hd)
        y_b, st_p = _ret_prompt(_sample_rows_base(y_b_s[:dec_batch], n_prompt), z_b, ret_gn_g, ret_gn_b, l,
                                batch=batch, seq=seq, heads=heads_b, dk=dk, dv=dv)
        y_c = _chunk_mix(_sample_rows_base(y_c_s[:dec_batch], n_prompt), vn, u_c, w_spatial, bs_t, l,
                         rows=n_prompt, groups=groups, cg=cg)

        y = _merge(o_a, y_b, y_c, gates, w_up_a16, w_up_b16, w_up_c16, l, tm=tm)
        x = _out_proj(x, y, w_out16, ln_g3, ln_b3, l, 3 * l + 1, alpha=alpha)
        x = _ffn(x, ffb_gu16, ffb_down16, ln_g3, ln_b3, l, 3 * l + 2, alpha=alpha, tm=tm)

        outs[0].append(z_a[:n_prompt, wa:2 * wa].reshape(batch, seq, heads_a, hd))
        outs[1].append(z_a[:n_prompt, 2 * wa:].reshape(batch, seq, heads_a, hd))
        outs[2].append(lf[:n_prompt, :heads_a].reshape(batch, seq, heads_a))
        outs[3].append(st_p)
        outs[4].append(z_a[s0:s1, wa:2 * wa].reshape(dec_batch, 1, heads_a, hd))
        outs[5].append(z_a[s0:s1, 2 * wa:].reshape(dec_batch, 1, heads_a, hd))
        outs[6].append(lf[s0:s1, :heads_a].reshape(dec_batch, 1, heads_a))
        outs[7].append(st_s)
        outs[8].append(vn[s0:s1].reshape(dec_batch, 1, groups, cg))

    y_prompt = x[:n_prompt].reshape(batch, seq, d)
    y_sample = x[n_prompt:n_prompt + dec_batch].reshape(dec_batch, 1, d)
    return (y_prompt, y_sample) + tuple(jnp.stack(o) for o in outs)
```

```python
import functools
import math

import jax
import jax.numpy as jnp
from jax import lax
from jax.experimental import pallas as pl
from jax.experimental.pallas import tpu as pltpu

F32 = jnp.float32
BF16 = jnp.bfloat16

LN_EPS = 1e-5
ROPE_BASE = 10000.0
NEG_BIG = -1e30
LANES = 128
SAMPLE_ROWS = 16
PROJ_TN = 1024
V7X_VMEM_LIMIT = 56 * 1024 * 1024


def _pick_tile(n, max_tile, mult):
    best = None
    for t in range(mult, min(n, max_tile) + 1, mult):
        if n % t == 0:
            best = t
    assert best is not None, (n, max_tile, mult)
    return best


def _params(sem):
    return pltpu.CompilerParams(dimension_semantics=sem, vmem_limit_bytes=V7X_VMEM_LIMIT)


def _layer_norm(y, g, b):
    mu = jnp.mean(y, axis=-1, keepdims=True)
    d = y - mu
    var = jnp.mean(d * d, axis=-1, keepdims=True)
    return d * lax.rsqrt(var + LN_EPS) * g + b


def _silu(x):
    return x * jax.nn.sigmoid(x)


def _gelu_tanh(x):
    c = math.sqrt(2.0 / math.pi)
    return 0.5 * x * (1.0 + jnp.tanh(c * (x + 0.044715 * (x * x * x))))


def _log_sigmoid(x):
    return jnp.minimum(x, 0.0) - jnp.log1p(jnp.exp(-jnp.abs(x)))


def _split3(x):
    hi = x.astype(BF16)
    r = x - hi.astype(F32)
    mid = r.astype(BF16)
    lo = (r - mid.astype(F32)).astype(BF16)
    return hi, mid, lo


def _ffn_kernel(x_ref, wg_ref, wu_ref, wd_ref, g_ref, b_ref, o_ref, xb_ref, *, alpha, nf):
    f = pl.program_id(1)

    @pl.when(f == 0)
    def _():
        xb_ref[...] = x_ref[...].astype(BF16)
        o_ref[...] = jnp.zeros_like(o_ref)

    xb = xb_ref[...]
    hg = jnp.dot(xb, wg_ref[...], preferred_element_type=F32)
    hu = jnp.dot(xb, wu_ref[...], preferred_element_type=F32)
    a = (_silu(hg) * hu).astype(BF16)
    o_ref[...] += jnp.dot(a, wd_ref[...], preferred_element_type=F32)

    @pl.when(f == nf - 1)
    def _():
        o_ref[...] = _layer_norm(alpha * x_ref[...] + 0.5 * o_ref[...], g_ref[...], b_ref[...])


def _ffn(x, w_gu, w_down, ln_g, ln_b, l, ln_idx, *, alpha, tm):
    mt, d = x.shape
    dff = w_down.shape[1]
    tf = _pick_tile(dff, 512, LANES)
    nf = dff // tf
    return pl.pallas_call(
        functools.partial(_ffn_kernel, alpha=alpha, nf=nf),
        grid=(mt // tm, nf),
        in_specs=[
            pl.BlockSpec((tm, d), lambda m, f: (m, 0), pipeline_mode=pl.Buffered(1)),
            pl.BlockSpec((None, d, tf), lambda m, f: (l, 0, f)),
            pl.BlockSpec((None, d, tf), lambda m, f: (l, 0, f + nf)),
            pl.BlockSpec((None, tf, d), lambda m, f: (l, f, 0)),
            pl.BlockSpec((None, 1, d), lambda m, f: (ln_idx, 0, 0)),
            pl.BlockSpec((None, 1, d), lambda m, f: (ln_idx, 0, 0)),
        ],
        out_specs=pl.BlockSpec((tm, d), lambda m, f: (m, 0)),
        out_shape=jax.ShapeDtypeStruct((mt, d), F32),
        scratch_shapes=[pltpu.VMEM((tm, d), BF16)],
        compiler_params=_params(("parallel", "arbitrary")),
        name="ffn",
    )(x, w_gu, w_gu, w_down, ln_g, ln_b)


def _proj_a_kernel(x_ref, w_ref, wf_ref, bf_ref, z_ref, lf_ref, xb_ref, kv_ref):
    n = pl.program_id(1)

    @pl.when(n == 0)
    def _():
        xb0 = x_ref[...].astype(BF16)
        xb_ref[...] = xb0
        fa = jnp.dot(xb0, wf_ref[...], preferred_element_type=F32) + bf_ref[...]
        lf_ref[...] = _log_sigmoid(fa)

    z = jnp.dot(xb_ref[...], w_ref[...], preferred_element_type=F32)
    z_ref[...] = z

    @pl.when(n >= 1)
    def _():
        kv_ref[...] = z.astype(BF16)


def _proj_a(x, w16, w_f, b_f, l, *, wa, tm):
    mt, d = x.shape
    return pl.pallas_call(
        _proj_a_kernel,
        grid=(mt // tm, 3),
        in_specs=[
            pl.BlockSpec((tm, d), lambda m, n: (m, 0)),
            pl.BlockSpec((None, d, wa), lambda m, n: (l, 0, n)),
            pl.BlockSpec((None, d, LANES), lambda m, n: (l, 0, 0)),
            pl.BlockSpec((None, 1, LANES), lambda m, n: (l, 0, 0)),
        ],
        out_specs=[
            pl.BlockSpec((tm, wa), lambda m, n: (m, n)),
            pl.BlockSpec((tm, LANES), lambda m, n: (m, 0)),
            pl.BlockSpec((tm, d), lambda m, n: (m, 0)),
            pl.BlockSpec((tm, wa), lambda m, n: (m, jnp.maximum(n - 1, 0))),
        ],
        out_shape=[jax.ShapeDtypeStruct((mt, 3 * wa), F32), jax.ShapeDtypeStruct((mt, LANES), F32),
                   jax.ShapeDtypeStruct((mt, d), BF16), jax.ShapeDtypeStruct((mt, 2 * wa), BF16)],
        compiler_params=_params(("parallel", "arbitrary")),
        name="proj_a",
    )(x, w16, w_f, b_f)


def _proj_b_kernel(xb_ref, w_ref, cos_ref, sin_ref, z_ref, *, n_v, dk, kscale):
    n = pl.program_id(1)
    acc = jnp.dot(xb_ref[...], w_ref[...], preferred_element_type=F32)
    n_heads2 = acc.shape[1] // dk

    @pl.when(n == 0)
    def _():
        cos = cos_ref[...]
        sin = sin_ref[...]
        for h in range(n_heads2):
            xh = acc[:, h * dk:(h + 1) * dk]
            r = xh * cos + pltpu.roll(xh, dk // 2, 1) * sin
            if h >= n_heads2 // 2:
                r = r * kscale
            z_ref[:, h * dk:(h + 1) * dk] = r.astype(BF16)

    @pl.when(jnp.logical_and(n >= 1, n < 1 + n_v))
    def _():
        z_ref[...] = acc.astype(BF16)

    @pl.when(n >= 1 + n_v)
    def _():
        z_ref[...] = _silu(acc).astype(BF16)


def _proj_b(xb, w_main, cos_t, sin_t, l, *, col0, wqk, wv, dk, tm):
    mt, d = xb.shape
    assert dk == LANES
    tn = 2 * wqk
    assert wv % tn == 0 and col0 % tn == 0
    ncols = 2 * wqk + 2 * wv
    kern = functools.partial(_proj_b_kernel, n_v=wv // tn, dk=dk, kscale=dk ** -0.5)
    return pl.pallas_call(
        kern,
        grid=(mt // tm, ncols // tn),
        in_specs=[
            pl.BlockSpec((tm, d), lambda m, n: (m, 0)),
            pl.BlockSpec((None, d, tn), lambda m, n: (l, 0, n + col0 // tn)),
            pl.BlockSpec((tm, dk), lambda m, n: (m, 0)),
            pl.BlockSpec((tm, dk), lambda m, n: (m, 0)),
        ],
        out_specs=pl.BlockSpec((tm, tn), lambda m, n: (m, n)),
        out_shape=jax.ShapeDtypeStruct((mt, ncols), BF16),
        compiler_params=_params(("parallel", "arbitrary")),
        name="proj_b",
    )(xb, w_main, cos_t, sin_t)


def _proj_c_kernel(xb_ref, w_ref, g_ref, b_ref, u_ref, vn_ref):
    n = pl.program_id(1)
    acc = _gelu_tanh(jnp.dot(xb_ref[...], w_ref[...], preferred_element_type=F32))

    @pl.when(n == 0)
    def _():
        u_ref[...] = acc.astype(BF16)

    @pl.when(n == 1)
    def _():
        vn_ref[...] = _layer_norm(acc, g_ref[...], b_ref[...])


def _proj_c(xb, w_main, vln_g, vln_b, l, *, col0, wc, tm):
    mt, d = xb.shape
    assert col0 % wc == 0
    return pl.pallas_call(
        _proj_c_kernel,
        grid=(mt // tm, 2),
        in_specs=[
            pl.BlockSpec((tm, d), lambda m, n: (m, 0)),
            pl.BlockSpec((None, d, wc), lambda m, n: (l, 0, n + col0 // wc)),
            pl.BlockSpec((None, 1, wc), lambda m, n: (l, 0, 0)),
            pl.BlockSpec((None, 1, wc), lambda m, n: (l, 0, 0)),
        ],
        out_specs=[
            pl.BlockSpec((tm, wc), lambda m, n: (m, 0)),
            pl.BlockSpec((tm, wc), lambda m, n: (m, 0)),
        ],
        out_shape=[jax.ShapeDtypeStruct((mt, wc), BF16), jax.ShapeDtypeStruct((mt, wc), F32)],
        compiler_params=_params(("parallel", "arbitrary")),
        name="proj_c",
    )(xb, w_main, vln_g, vln_b)


def _proj_gate_kernel(xb_ref, w_ref, z_ref):
    acc = jnp.dot(xb_ref[...], w_ref[...], preferred_element_type=F32)
    z_ref[...] = jax.nn.sigmoid(acc).astype(BF16)


def _proj_gate(xb, w_main, l, *, col0, ncols, tm):
    mt, d = xb.shape
    tn = _pick_tile(math.gcd(ncols, col0), PROJ_TN, LANES)
    return pl.pallas_call(
        _proj_gate_kernel,
        grid=(mt // tm, ncols // tn),
        in_specs=[
            pl.BlockSpec((tm, d), lambda m, n: (m, 0)),
            pl.BlockSpec((None, d, tn), lambda m, n: (l, 0, n + col0 // tn)),
        ],
        out_specs=pl.BlockSpec((tm, tn), lambda m, n: (m, n)),
        out_shape=jax.ShapeDtypeStruct((mt, ncols), BF16),
        compiler_params=_params(("parallel", "arbitrary")),
        name="proj_gate",
    )(xb, w_main)


def _rope_kernel(pos_ref, cos_ref, sin_ref, *, half):
    lane = lax.broadcasted_iota(jnp.int32, pos_ref.shape, 1)
    j = jnp.where(lane >= half, lane - half, lane).astype(F32)
    inv = jnp.exp(j * (-math.log(ROPE_BASE) / half))
    ang = pos_ref[...] * inv
    cos_ref[...] = jnp.cos(ang)
    sin_ref[...] = jnp.where(lane >= half, 1.0, -1.0) * jnp.sin(ang)


def _rope_tables(pos_b, *, tm):
    mt, dk = pos_b.shape
    spec = pl.BlockSpec((tm, dk), lambda m: (m, 0))
    return pl.pallas_call(
        functools.partial(_rope_kernel, half=dk // 2),
        grid=(mt // tm,),
        in_specs=[spec],
        out_specs=[spec, spec],
        out_shape=[jax.ShapeDtypeStruct((mt, dk), F32)] * 2,
        compiler_params=_params(("parallel",)),
        name="rope_tables",
    )(pos_b)


def _cumsum_kernel(lf_ref, c_ref, carry_ref):
    @pl.when(pl.program_id(1) == 0)
    def _():
        carry_ref[...] = jnp.zeros_like(carry_ref)

    x = lf_ref[...]
    tc = x.shape[0]
    row = lax.broadcasted_iota(jnp.int32, (tc, tc), 0)
    col = lax.broadcasted_iota(jnp.int32, (tc, tc), 1)
    tri = jnp.where(col <= row, 1.0, 0.0).astype(BF16)
    c = carry_ref[0:1, :]
    for part in _split3(x):
        c = c + jnp.dot(tri, part, preferred_element_type=F32)
    c_ref[...] = c
    carry_ref[...] = jnp.broadcast_to(c[tc - 1:tc, :], carry_ref.shape)


def _cumsum(lf, *, batch, seq):
    tc = _pick_tile(seq, 512, 8)
    nt = seq // tc
    return pl.pallas_call(
        _cumsum_kernel,
        grid=(batch, nt),
        in_specs=[pl.BlockSpec((tc, LANES), lambda b, t: (b * nt + t, 0))],
        out_specs=pl.BlockSpec((tc, LANES), lambda b, t: (b * nt + t, 0)),
        out_shape=jax.ShapeDtypeStruct((batch * seq, LANES), F32),
        scratch_shapes=[pltpu.VMEM((8, LANES), F32)],
        compiler_params=_params(("parallel", "arbitrary")),
        name="logf_cumsum",
    )(lf)


def _fox_prompt_kernel(base_ref, q_ref, kb_ref, vb_ref, ck_ref, o_ref, m_ref, l_ref, acc_ref,
                       *, tq, hps, hd, scale):
    del base_ref
    i = pl.program_id(2)
    log2e = math.log2(math.e)

    qs = [(q_ref[:, hh * hd:(hh + 1) * hd] * (scale * log2e)).astype(BF16) for hh in range(hps)]
    m_ref[...] = jnp.full_like(m_ref, NEG_BIG)
    l_ref[...] = jnp.zeros_like(l_ref)
    acc_ref[...] = jnp.zeros_like(acc_ref)

    def step(j, masked):
        off = pl.multiple_of(j * tq, tq)
        for hh in range(hps):
            k = kb_ref[pl.ds(off, tq), hh * hd:(hh + 1) * hd]
            v = vb_ref[pl.ds(off, tq), hh * hd:(hh + 1) * hd]
            s = lax.dot_general(qs[hh], k, (((1,), (1,)), ((), ())), preferred_element_type=F32)
            s = s - ck_ref[hh, j] * log2e
            if masked:
                row = lax.broadcasted_iota(jnp.int32, s.shape, 0)
                col = lax.broadcasted_iota(jnp.int32, s.shape, 1)
                s = jnp.where(col <= row, s, NEG_BIG)
            m_prev = m_ref[hh]
            m_new = jnp.maximum(m_prev, jnp.max(s, axis=1, keepdims=True))
            p = jnp.exp2(s - jnp.tile(m_new, (1, tq // LANES)))
            alpha = jnp.exp2(m_prev - m_new)
            l_ref[hh] = alpha * l_ref[hh] + jnp.sum(p, axis=1, keepdims=True)
            acc_ref[hh] = alpha * acc_ref[hh] + jnp.dot(p.astype(BF16), v, preferred_element_type=F32)
            m_ref[hh] = m_new

    def body(j, carry):
        step(j, False)
        return carry

    lax.fori_loop(0, i, body, 0)
    step(i, True)
    for hh in range(hps):
        o_ref[:, hh * hd:(hh + 1) * hd] = (acc_ref[hh] / l_ref[hh]).astype(BF16)


def _fox_prompt(base, z_a, kv16, ck, *, batch, seq, heads, hd):
    assert hd == LANES
    tq = _pick_tile(seq, 512, LANES)
    nq = seq // tq
    hps = _pick_tile(heads, 4, 1)
    wb = hps * hd
    ng = heads // hps
    kern = functools.partial(_fox_prompt_kernel, tq=tq, hps=hps, hd=hd, scale=hd ** -0.5)
    return pl.pallas_call(
        kern,
        grid=(batch, ng, nq),
        in_specs=[
            pl.BlockSpec(memory_space=pl.ANY),
            pl.BlockSpec((tq, wb), lambda b, h, i: (b * nq + i, h)),
            pl.BlockSpec((seq, wb), lambda b, h, i: (b, h)),
            pl.BlockSpec((seq, wb), lambda b, h, i: (b, ng + h)),
            pl.BlockSpec((hps, nq, 1, tq), lambda b, h, i: (b * ng + h, 0, 0, 0)),
        ],
        out_specs=pl.BlockSpec((tq, wb), lambda b, h, i: (b * nq + i, h)),
        out_shape=jax.ShapeDtypeStruct(base.shape, base.dtype),
        input_output_aliases={0: 0},
        scratch_shapes=[
            pltpu.VMEM((hps, tq, LANES), F32),
            pltpu.VMEM((hps, tq, LANES), F32),
            pltpu.VMEM((hps, tq, hd), F32),
        ],
        compiler_params=_params(("parallel", "parallel", "arbitrary")),
        name="fox_prompt",
    )(base, z_a, kv16, kv16, ck)


def _fox_sample_kernel(pt_ref, qkv_ref, lfn_ref, ck_hbm, cv_hbm, lf_hbm, o_ref,
                       kbuf, vbuf, lbuf, sem, qs_ref, m_ref, l_ref, acc_ref, carry_ref,
                       *, layer, heads, hd, scale, n_pages, gp, dec_batch):
    b = pl.program_id(0)
    g = pl.program_id(1)
    ng = n_pages // gp
    step = b * ng + g
    slot = step % 2
    rows = lbuf.shape[2]

    def copies(bb, gg, sl):
        out = []
        for p in range(gp):
            pid = pt_ref[bb, n_pages - 1 - (gg * gp + p)]
            out.append(pltpu.make_async_copy(ck_hbm.at[layer, pid], kbuf.at[sl, p], sem.at[0, sl]))
            out.append(pltpu.make_async_copy(cv_hbm.at[layer, pid], vbuf.at[sl, p], sem.at[1, sl]))
            out.append(pltpu.make_async_copy(lf_hbm.at[layer, pid], lbuf.at[sl, p], sem.at[2, sl]))
        return out

    @pl.when(step == 0)
    def _():
        for c in copies(b, g, slot):
            c.start()

    @pl.when(step + 1 < dec_batch * ng)
    def _():
        wrap = g + 1 == ng
        for c in copies(jnp.where(wrap, b + 1, b), jnp.where(wrap, 0, g + 1), 1 - slot):
            c.start()

    lane = lax.broadcasted_iota(jnp.int32, (LANES, LANES), 1)
    src = lax.broadcasted_iota(jnp.int32, (LANES, LANES), 0)
    same_head = (src % heads) == (lane % heads)

    @pl.when(g == 0)
    def _():
        q = qkv_ref[0:heads, :] * scale
        qs_ref[...] = q.astype(BF16)
        m_ref[...] = jnp.sum(q * qkv_ref[heads:2 * heads, :], axis=1, keepdims=True)
        l_ref[...] = jnp.ones_like(l_ref)
        acc_ref[...] = qkv_ref[2 * heads:3 * heads, :]
        spread = jnp.where(src == lane % heads, 1.0, 0.0).astype(BF16)
        carry = jnp.zeros((1, LANES), F32)
        for part in _split3(lfn_ref[...]):
            carry = carry + jnp.dot(part, spread, preferred_element_type=F32)
        carry_ref[...] = carry

    for c in copies(b, g, slot):
        c.wait()

    later_w = jnp.where(jnp.logical_and(same_head, src > lane), 1.0, 0.0).astype(BF16)
    whole_w = jnp.where(same_head, 1.0, 0.0).astype(BF16)
    mix_w = jnp.concatenate([later_w, whole_w], axis=1)
    lf_all = lbuf[slot].reshape(gp * rows, LANES)
    sums = jnp.zeros((gp * rows, 2 * LANES), F32)
    for part in _split3(lf_all):
        sums = sums + jnp.dot(part, mix_w, preferred_element_type=F32)

    row_id = lax.broadcasted_iota(jnp.int32, (rows, LANES), 0)
    valid = (lax.broadcasted_iota(jnp.int32, (heads, LANES), 1) % heads
             == lax.broadcasted_iota(jnp.int32, (heads, LANES), 0))
    qs = qs_ref[...]
    carry = carry_ref[...]
    tiles = []
    for p in range(gp):
        within = sums[p * rows:(p + 1) * rows, :LANES]
        tot = sums[p * rows:(p + 1) * rows, LANES:]
        suf = tot
        sh = 1
        while sh < rows:
            suf = suf + jnp.where(row_id < rows - sh, pltpu.roll(suf, rows - sh, 0), 0.0)
            sh *= 2
        bias = carry + within + (suf - tot)
        carry = carry + suf[0:1, :]
        k2 = kbuf[slot, p].reshape(rows * LANES, hd).astype(BF16)
        s = lax.dot_general(qs, k2, (((1,), (1,)), ((), ())), preferred_element_type=F32)
        for r in range(rows):
            t = s[:, r * LANES:(r + 1) * LANES] + jnp.broadcast_to(bias[r:r + 1, :], (heads, LANES))
            tiles.append(jnp.where(valid, t, NEG_BIG))
    carry_ref[...] = carry

    m_tile = tiles[0]
    for t in tiles[1:]:
        m_tile = jnp.maximum(m_tile, t)
    m_prev = m_ref[...]
    m_new = jnp.maximum(m_prev, jnp.max(m_tile, axis=1, keepdims=True))
    alpha = jnp.exp(m_prev - m_new)
    probs = [jnp.exp(t - m_new) for t in tiles]
    p_sum = probs[0]
    for t in probs[1:]:
        p_sum = p_sum + t
    l_ref[...] = alpha * l_ref[...] + jnp.sum(p_sum, axis=1, keepdims=True)
    acc = alpha * acc_ref[...]
    for p in range(gp):
        pp = jnp.concatenate(probs[p * rows:(p + 1) * rows], axis=1).astype(BF16)
        v2 = vbuf[slot, p].reshape(rows * LANES, hd).astype(BF16)
        acc = acc + jnp.dot(pp, v2, preferred_element_type=F32)
    acc_ref[...] = acc
    m_ref[...] = m_new

    @pl.when(g == ng - 1)
    def _():
        o_ref[...] = acc_ref[...] / l_ref[...]


def _fox_sample(qkv_s, lf_s, cache_k, cache_v, cache_lf, page_table, l, *, heads, hd):
    dec_batch = qkv_s.shape[0]
    n_pages = page_table.shape[1]
    page = cache_k.shape[2]
    rows = cache_lf.shape[2]
    assert LANES % heads == 0 and rows * LANES == page * heads
    gp = _pick_tile(n_pages, 8, 1)
    kern = functools.partial(_fox_sample_kernel, layer=l, heads=heads, hd=hd, scale=hd ** -0.5,
                             n_pages=n_pages, gp=gp, dec_batch=dec_batch)
    grid_spec = pltpu.PrefetchScalarGridSpec(
        num_scalar_prefetch=1,
        grid=(dec_batch, n_pages // gp),
        in_specs=[
            pl.BlockSpec((None, 3 * heads, hd), lambda b, g, pt: (b, 0, 0)),
            pl.BlockSpec((None, 1, LANES), lambda b, g, pt: (b, 0, 0)),
            pl.BlockSpec(memory_space=pl.ANY),
            pl.BlockSpec(memory_space=pl.ANY),
            pl.BlockSpec(memory_space=pl.ANY),
        ],
        out_specs=pl.BlockSpec((None, heads, hd), lambda b, g, pt: (b, 0, 0)),
        scratch_shapes=[
            pltpu.VMEM((2, gp, page, heads, hd), F32),
            pltpu.VMEM((2, gp, page, heads, hd), F32),
            pltpu.VMEM((2, gp, rows, LANES), F32),
            pltpu.SemaphoreType.DMA((3, 2)),
            pltpu.VMEM((heads, hd), BF16),
            pltpu.VMEM((heads, 1), F32),
            pltpu.VMEM((heads, 1), F32),
            pltpu.VMEM((heads, hd), F32),
            pltpu.VMEM((1, LANES), F32),
        ],
    )
    return pl.pallas_call(
        kern,
        grid_spec=grid_spec,
        out_shape=jax.ShapeDtypeStruct((dec_batch, heads, hd), F32),
        compiler_params=_params(("arbitrary", "arbitrary")),
        name="fox_sample",
    )(page_table, qkv_s, lf_s, cache_k, cache_v, cache_lf)


def _log_gammas(n_heads):
    return [math.log1p(-(2.0 ** (-5.0 - h))) for h in range(n_heads)]


def _group_norm_gate(o, gate, g, b):
    mu = jnp.mean(o, axis=-1, keepdims=True)
    d = o - mu
    var = jnp.mean(d * d, axis=-1, keepdims=True)
    return gate * (d * lax.rsqrt(var + LN_EPS) * g + b)


def _ret_prompt_kernel(base_ref, q_ref, k_ref, v_ref, gate_ref, g_ref, b_ref, y_ref, st_ref,
                       *, heads, dk, dv, chunk):
    del base_ref

    @pl.when(pl.program_id(1) == 0)
    def _():
        st_ref[...] = jnp.zeros_like(st_ref)

    n_col = lax.broadcasted_iota(jnp.int32, (chunk, 1), 0).astype(F32)
    diff = (lax.broadcasted_iota(jnp.int32, (chunk, chunk), 0)
            - lax.broadcasted_iota(jnp.int32, (chunk, chunk), 1)).astype(F32)
    for h, lg in enumerate(_log_gammas(heads)):
        q = q_ref[:, h * dk:(h + 1) * dk]
        k = k_ref[:, h * dk:(h + 1) * dk]
        v = v_ref[:, h * dv:(h + 1) * dv]
        state = st_ref[h]
        dmask = jnp.where(diff >= 0, jnp.exp(lg * jnp.maximum(diff, 0.0)), 0.0)
        a = lax.dot_general(q, k, (((1,), (1,)), ((), ())), preferred_element_type=F32) * dmask
        o = jnp.dot(a.astype(BF16), v, preferred_element_type=F32)
        o = o + jnp.dot(q, state.astype(BF16), preferred_element_type=F32) * jnp.exp(lg * (n_col + 1.0))
        kdec = k.astype(F32) * jnp.exp(lg * (chunk - 1.0 - n_col))
        st_ref[h] = state * math.exp(lg * chunk) + jnp.dot(kdec.T.astype(BF16), v, preferred_element_type=F32)
        gate = gate_ref[:, h * dv:(h + 1) * dv].astype(F32)
        y = _group_norm_gate(o, gate, g_ref[h:h + 1, :], b_ref[h:h + 1, :])
        y_ref[:, h * dv:(h + 1) * dv] = y.astype(BF16)


def _ret_prompt(base, z_b, gn_g, gn_b, l, *, batch, seq, heads, dk, dv):
    chunk = LANES
    nc = seq // chunk
    wqk, wv = heads * dk, heads * dv
    assert wv % wqk == 0
    kern = functools.partial(_ret_prompt_kernel, heads=heads, dk=dk, dv=dv, chunk=chunk)
    return pl.pallas_call(
        kern,
        grid=(batch, nc),
        in_specs=[
            pl.BlockSpec(memory_space=pl.ANY),
            pl.BlockSpec((chunk, wqk), lambda b, c: (b * nc + c, 0)),
            pl.BlockSpec((chunk, wqk), lambda b, c: (b * nc + c, 1)),
            pl.BlockSpec((chunk, wv), lambda b, c: (b * nc + c, 2 * wqk // wv)),
            pl.BlockSpec((chunk, wv), lambda b, c: (b * nc + c, 2 * wqk // wv + 1)),
            pl.BlockSpec((None, heads, dv), lambda b, c: (l, 0, 0)),
            pl.BlockSpec((None, heads, dv), lambda b, c: (l, 0, 0)),
        ],
        out_specs=[
            pl.BlockSpec((chunk, wv), lambda b, c: (b * nc + c, 0)),
            pl.BlockSpec((None, heads, dk, dv), lambda b, c: (b, 0, 0, 0)),
        ],
        out_shape=[
            jax.ShapeDtypeStruct(base.shape, base.dtype),
            jax.ShapeDtypeStruct((batch, heads, dk, dv), F32),
        ],
        input_output_aliases={0: 0},
        compiler_params=_params(("parallel", "arbitrary")),
        name="ret_prompt",
    )(base, z_b, z_b, z_b, z_b, gn_g, gn_b)


def _sample_mix_kernel(zb_ref, st_ref, g_ref, b_ref, u_ref, vn_ref, ws_ref, bs_ref, y_ref, yc_ref, nst_ref,
                       *, heads, dk, dv):
    b = pl.program_id(0)

    @pl.when(b == 0)
    def _():
        y_ref[...] = jnp.zeros_like(y_ref)
        yc_ref[...] = u_ref[...].astype(F32) * (ws_ref[...] * vn_ref[...] + bs_ref[...])

    wqk, wv = heads * dk, heads * dv
    blk = zb_ref[...].astype(F32)
    pick = lax.broadcasted_iota(jnp.int32, blk.shape, 0) == b
    row = jnp.sum(jnp.where(pick, blk, 0.0), axis=0, keepdims=True)
    eye =(lax.broadcasted_iota(jnp.int32, (dk, dk), 0) == lax.broadcasted_iota(jnp.int32, (dk, dk), 1))
    for h, lg in enumerate(_log_gammas(heads)):
        q = row[:, h * dk:(h + 1) * dk]
        k = row[:, wqk + h * dk:wqk + (h + 1) * dk]
        v = row[:, 2 * wqk + h * dv:2 * wqk + (h + 1) * dv]
        gate = row[:, 2 * wqk + wv + h * dv:2 * wqk + wv + (h + 1) * dv]
        qcol = jnp.sum(jnp.where(eye, jnp.broadcast_to(q, (dk, dk)), 0.0), axis=1, keepdims=True)
        kcol = jnp.sum(jnp.where(eye, jnp.broadcast_to(k, (dk, dk)), 0.0), axis=1, keepdims=True)
        state = st_ref[h]
        gamma = math.exp(lg)
        new_state = state * gamma + kcol * v
        nst_ref[h] = new_state
        o = jnp.sum(q * k, axis=1, keepdims=True) * v + jnp.sum(qcol * state, axis=0, keepdims=True) * gamma
        y = _group_norm_gate(o, gate, g_ref[h:h + 1, :], b_ref[h:h + 1, :])
        y_ref[pl.ds(b, 1), h * dv:(h + 1) * dv] = y


def _sample_mix(z_b, state_ret, gn_g, gn_b, u_c, vn, ws_row, bs_row, l, *, row0, dec_batch, heads, dk, dv):
    wv = heads * dv
    wc = u_c.shape[1]
    rb = row0 // SAMPLE_ROWS
    kern = functools.partial(_sample_mix_kernel, heads=heads, dk=dk, dv=dv)
    return pl.pallas_call(
        kern,
        grid=(dec_batch,),
        in_specs=[
            pl.BlockSpec((SAMPLE_ROWS, z_b.shape[1]), lambda b: (rb, 0)),
            pl.BlockSpec((None, None, heads, dk, dv), lambda b: (l, b, 0, 0, 0)),
            pl.BlockSpec((None, heads, dv), lambda b: (l, 0, 0)),
            pl.BlockSpec((None, heads, dv), lambda b: (l, 0, 0)),
            pl.BlockSpec((SAMPLE_ROWS, wc), lambda b: (rb, 0)),
            pl.BlockSpec((SAMPLE_ROWS, wc), lambda b: (rb, 0)),
            pl.BlockSpec((None, 1, wc), lambda b: (l, 0, 0)),
            pl.BlockSpec((None, 1, wc), lambda b: (l, 0, 0)),
        ],
        out_specs=[
            pl.BlockSpec((SAMPLE_ROWS, wv), lambda b: (0, 0)),
            pl.BlockSpec((SAMPLE_ROWS, wc), lambda b: (0, 0)),
            pl.BlockSpec((None, heads, dk, dv), lambda b: (b, 0, 0, 0)),
        ],
        out_shape=[
            jax.ShapeDtypeStruct((SAMPLE_ROWS, wv), F32),
            jax.ShapeDtypeStruct((SAMPLE_ROWS, wc), F32),
            jax.ShapeDtypeStruct((dec_batch, heads, dk, dv), F32),
        ],
        compiler_params=_params(("arbitrary",)),
        name="sample_mix",
    )(z_b, state_ret, gn_g, gn_b, u_c, vn, ws_row, bs_row)


def _chunk_mix_kernel(base_ref, vn_ref, u_ref, ws_ref, bst_ref, y_ref, *, groups, cg):
    del base_ref
    chunk = ws_ref.shape[1]
    lower = (lax.broadcasted_iota(jnp.int32, (chunk, chunk), 1)
             <= lax.broadcasted_iota(jnp.int32, (chunk, chunk), 0))
    bst = bst_ref[...]
    for g in range(groups):
        w = jnp.where(lower, ws_ref[g], 0.0).astype(BF16)
        cols = slice(g * cg, (g + 1) * cg)
        for c in range(vn_ref.shape[0] // chunk):
            rows = slice(c * chunk, (c + 1) * chunk)
            s = jnp.dot(w, vn_ref[rows, cols].astype(BF16), preferred_element_type=F32) + bst[:, g:g + 1]
            y_ref[rows, cols] = (u_ref[rows, cols].astype(F32) * s).astype(BF16)


def _chunk_mix(base, vn, u_c, w_spatial, bs_t, l, *, rows, groups, cg):
    chunk = w_spatial.shape[2]
    wc = groups * cg
    tr = _pick_tile(rows, 4 * chunk, chunk)
    kern = functools.partial(_chunk_mix_kernel, groups=groups, cg=cg)
    return pl.pallas_call(
        kern,
        grid=(rows // tr,),
        in_specs=[
            pl.BlockSpec(memory_space=pl.ANY),
            pl.BlockSpec((tr, wc), lambda n: (n, 0)),
            pl.BlockSpec((tr, wc), lambda n: (n, 0)),
            pl.BlockSpec((None, groups, chunk, chunk), lambda n: (l, 0, 0, 0)),
            pl.BlockSpec((None, chunk, groups), lambda n: (l, 0, 0)),
        ],
        out_specs=pl.BlockSpec((tr, wc), lambda n: (n, 0)),
        out_shape=jax.ShapeDtypeStruct(base.shape, base.dtype),
        input_output_aliases={0: 0},
        compiler_params=_params(("parallel",)),
        name="chunk_mix",
    )(base, vn, u_c, w_spatial, bs_t)


def _merge_kernel(oa_ref, yb_ref, yc_ref, ga_ref, gb_ref, gc_ref, wa_ref, wb_ref, wc_ref, y_ref):
    ya = jnp.dot(oa_ref[...], wa_ref[...], preferred_element_type=F32)
    yb = jnp.dot(yb_ref[...], wb_ref[...], preferred_element_type=F32)
    yc = jnp.dot(yc_ref[...], wc_ref[...], preferred_element_type=F32)
    y = ga_ref[...].astype(F32) * ya + gb_ref[...].astype(F32) * yb + gc_ref[...].astype(F32) * yc
    y_ref[...] = y.astype(BF16)


def _merge(o_a, y_b, y_c, gates, w_up_a, w_up_b, w_up_c, l, *, tm):
    mt = o_a.shape[0]
    d = w_up_a.shape[2]
    tn = _pick_tile(d, PROJ_TN, LANES)
    nn = d // tn

    def act(a):
        return pl.BlockSpec((tm, a.shape[1]), lambda m, n: (m, 0))

    def wgt(w):
        return pl.BlockSpec((None, w.shape[1], tn), lambda m, n: (l, 0, n))

    return pl.pallas_call(
        _merge_kernel,
        grid=(mt // tm, nn),
        in_specs=[
            act(o_a), act(y_b), act(y_c),
            pl.BlockSpec((tm, tn), lambda m, n: (m, n)),
            pl.BlockSpec((tm, tn), lambda m, n: (m, n + nn)),
            pl.BlockSpec((tm, tn), lambda m, n: (m, n + 2 * nn)),
            wgt(w_up_a), wgt(w_up_b), wgt(w_up_c),
        ],
        out_specs=pl.BlockSpec((tm, tn), lambda m, n: (m, n)),
        out_shape=jax.ShapeDtypeStruct((mt, d), BF16),
        compiler_params=_params(("parallel", "arbitrary")),
        name="merge",
    )(o_a, y_b, y_c, gates, gates, gates, w_up_a, w_up_b, w_up_c)


def _out_kernel(x_ref, y_ref, w_ref, g_ref, b_ref, o_ref, *, alpha):
    sub = jnp.dot(y_ref[...], w_ref[...], preferred_element_type=F32)
    o_ref[...] = _layer_norm(alpha * x_ref[...] + sub, g_ref[...], b_ref[...])


def _out_proj(x, y, w_out, ln_g, ln_b, l, ln_idx, *, alpha):
    mt, d = x.shape
    tm = _pick_tile(mt, 512, SAMPLE_ROWS)
    row = pl.BlockSpec((tm, d), lambda m: (m, 0))
    return pl.pallas_call(
        functools.partial(_out_kernel, alpha=alpha),
        grid=(mt // tm,),
        in_specs=[
            row, row,
            pl.BlockSpec((None, d, d), lambda m: (l, 0, 0)),
            pl.BlockSpec((None, 1, d), lambda m: (ln_idx, 0, 0)),
            pl.BlockSpec((None, 1, d), lambda m: (ln_idx, 0, 0)),
        ],
        out_specs=row,
        out_shape=jax.ShapeDtypeStruct((mt, d), F32),
        compiler_params=_params(("parallel",)),
        name="out_proj",
    )(x, y, w_out, ln_g, ln_b)


def _sample_rows_base(sample_rows, n_prompt):
    pad = SAMPLE_ROWS - sample_rows.shape[0]
    return jnp.pad(sample_rows.astype(BF16), ((n_prompt, pad), (0, 0)))


def kernel(x_prompt, x_sample, cache_k, cache_v, cache_logf, state_ret, page_table, w_in, b_forget, w_up_a, w_up_b, w_up_c, w_out, ret_gn_g, ret_gn_b, vln_g, vln_b, w_spatial, b_spatial, ffa_gu, ffa_down, ffb_gu, ffb_down, ln_g, ln_b):
    batch, seq, d = x_prompt.shape
    dec_batch, dec_seq, _ = x_sample.shape
    depth = w_in.shape[0]
    heads_a, hd = cache_k.shape[3], cache_k.shape[4]
    n_pool, page = cache_k.shape[1], cache_k.shape[2]
    heads_b, dk, dv = state_ret.shape[2], state_ret.shape[3], state_ret.shape[4]
    groups, chunk = w_spatial.shape[1], w_spatial.shape[2]
    wa, wqk, wv, wc = heads_a * hd, heads_b * dk, heads_b * dv, vln_g.shape[1]
    cg = wc // groups
    assert dec_seq == 1 and dec_batch <= SAMPLE_ROWS and heads_a <= LANES
    assert seq % chunk == 0 and chunk == LANES and (batch * seq) % SAMPLE_ROWS == 0
    assert w_in.shape[2] == 3 * wa + heads_a + 2 * wqk + 2 * wv + 2 * wc + 3 * d
    past_len = page_table.shape[1] * page
    alpha = (2 * depth) ** 0.25

    n_prompt = batch * seq
    mt = n_prompt + SAMPLE_ROWS
    tm = _pick_tile(mt, 1024, SAMPLE_ROWS)

    w16 = w_in.astype(BF16)
    w_rest = w16[:, :, 3 * wa + heads_a:]
    w_f =jnp.pad(w_in[:, :, 3 * wa:3 * wa + heads_a], ((0, 0), (0, 0), (0, LANES - heads_a))).astype(BF16)
    b_f = jnp.pad(b_forget, ((0, 0), (0, LANES - heads_a)))[:, None, :]
    ffa_gu16, ffa_down16 = ffa_gu.astype(BF16), ffa_down.astype(BF16)
    ffb_gu16, ffb_down16 = ffb_gu.astype(BF16), ffb_down.astype(BF16)
    w_up_a16, w_up_b16, w_up_c16 = w_up_a.astype(BF16), w_up_b.astype(BF16), w_up_c.astype(BF16)
    w_out16 = w_out.astype(BF16)
    ln_g3 = ln_g.reshape(depth * 3, 1, d)
    ln_b3 = ln_b.reshape(depth * 3, 1, d)
    vln_g3, vln_b3 = vln_g[:, None, :], vln_b[:, None, :]
    bs_t = jnp.swapaxes(b_spatial, 1, 2)
    ws_row = jnp.repeat(w_spatial[:, :, 0, 0], cg, axis=1)[:, None, :]
    bs_row = jnp.repeat(b_spatial[:, :, 0], cg, axis=1)[:, None, :]
    cache_lf = cache_logf.reshape(depth, n_pool, page * heads_a // LANES, LANES)

    pos = jnp.concatenate([jnp.tile(jnp.arange(seq, dtype=F32), batch), jnp.full((SAMPLE_ROWS,), past_len, F32)])
    cos_t, sin_t = _rope_tables(jnp.broadcast_to(pos[:, None], (mt, dk)), tm=tm)

    x = jnp.concatenate([x_prompt.reshape(n_prompt, d), x_sample.reshape(dec_batch, d),
                         jnp.zeros((SAMPLE_ROWS - dec_batch, d), F32)], axis=0)

    col_c = 2 * wqk + 2 * wv
    col_g = col_c + 2 * wc
    s0, s1 = n_prompt, n_prompt + dec_batch
    outs = [[] for _ in range(9)]
    for l in range(depth):
        x = _ffn(x, ffa_gu16, ffa_down16, ln_g3, ln_b3, l, 3 * l, alpha=alpha, tm=tm)

        z_a, lf, xb, kv16 = _proj_a(x, w16, w_f, b_f, l, wa=wa, tm=tm)
        z_b = _proj_b(xb, w_rest, cos_t, sin_t, l, col0=0, wqk=wqk, wv=wv, dk=dk, tm=tm)
        u_c, vn = _proj_c(xb, w_rest, vln_g3, vln_b3, l, col0=col_c, wc=wc, tm=tm)
        gates = _proj_gate(xb, w_rest, l, col0=col_g, ncols=3 * d, tm=tm)

        c = _cumsum(lf, batch=batch, seq=seq)
        tq = _pick_tile(seq, 512, LANES)
        ck = jnp.swapaxes(c[:, :heads_a].reshape(batch, seq, heads_a), 1, 2).reshape(batch * heads_a, seq // tq, 1, tq)
        o_a_s = _fox_sample(z_a[s0:s1].reshape(dec_batch, 3 * heads_a, hd), lf[s0:s1, None, :],
                            cache_k, cache_v, cache_lf, page_table, l, heads=heads_a, hd=hd)
        y_b_s, y_c_s, st_s = _sample_mix(z_b, state_ret, ret_gn_g, ret_gn_b, u_c, vn, ws_row, bs_row, l,
                                         row0=n_prompt, dec_batch=dec_batch, heads=heads_b, dk=dk, dv=dv)

        o_a = _fox_prompt(_sample_rows_base(o_a_s.reshape(dec_batch, wa), n_prompt), z_a, kv16, ck,
                          batch=batch, seq=seq, heads=heads_a, hd=hd)
        y_b, st_p = _ret_prompt(_sample_rows_base(y_b_s[:dec_batch], n_prompt), z_b, ret_gn_g, ret_gn_b, l,
                                batch=batch, seq=seq, heads=heads_b, dk=dk, dv=dv)
        y_c = _chunk_mix(_sample_rows_base(y_c_s[:dec_batch], n_prompt), vn, u_c, w_spatial, bs_t, l,
                         rows=n_prompt, groups=groups, cg=cg)

        y = _merge(o_a, y_b, y_c, gates, w_up_a16, w_up_b16, w_up_c16, l, tm=tm)
        x = _out_proj(x, y, w_out16, ln_g3, ln_b3, l, 3 * l + 1, alpha=alpha)
        x = _ffn(x, ffb_gu16, ffb_down16, ln_g3, ln_b3, l, 3 * l + 2, alpha=alpha, tm=tm)

        outs[0].append(z_a[:n_prompt, wa:2 * wa].reshape(batch, seq, heads_a, hd))
        outs[1].append(z_a[:n_prompt, 2 * wa:].reshape(batch, seq, heads_a, hd))
        outs[2].append(lf[:n_prompt, :heads_a].reshape(batch, seq, heads_a))
        outs[3].append(st_p)
        outs[4].append(z_a[s0:s1, wa:2 * wa].reshape(dec_batch, 1, heads_a, hd))
        outs[5].append(z_a[s0:s1, 2 * wa:].reshape(dec_batch, 1, heads_a, hd))
        outs[6].append(lf[s0:s1, :heads_a].reshape(dec_batch, 1, heads_a))
        outs[7].append(st_s)
        outs[8].append(vn[s0:s1].reshape(dec_batch, 1, groups, cg))

    y_prompt = x[:n_prompt].reshape(batch, seq, d)
    y_sample = x[n_prompt:n_prompt + dec_batch].reshape(dec_batch, 1, d)
    return (y_prompt, y_sample) + tuple(jnp.stack(o) for o in outs)
```

```python
import functools
import math

import jax
import jax.numpy as jnp
from jax import lax
from jax.experimental import pallas as pl
from jax.experimental.pallas import tpu as pltpu

F32 = jnp.float32
BF16 = jnp.bfloat16

LN_EPS = 1e-5
ROPE_BASE = 10000.0
NEG_BIG = -1e30
LANES = 128
SAMPLE_ROWS = 16
PROJ_TN = 1024
V7X_VMEM_LIMIT = 56 * 1024 * 1024


def _pick_tile(n, max_tile, mult):
    best = None
    for t in range(mult, min(n, max_tile) + 1, mult):
        if n % t == 0:
            best = t
    assert best is not None, (n, max_tile, mult)
    return best


def _params(sem):
    return pltpu.CompilerParams(dimension_semantics=sem, vmem_limit_bytes=V7X_VMEM_LIMIT)


def _layer_norm(y, g, b):
    mu = jnp.mean(y, axis=-1, keepdims=True)
    d = y - mu
    var = jnp.mean(d * d, axis=-1, keepdims=True)
    return d * lax.rsqrt(var + LN_EPS) * g + b


def _silu(x):
    return x * jax.nn.sigmoid(x)


def _gelu_tanh(x):
    c = math.sqrt(2.0 / math.pi)
    return 0.5 * x * (1.0 + jnp.tanh(c * (x + 0.044715 * (x * x * x))))


def _log_sigmoid(x):
    return jnp.minimum(x, 0.0) - jnp.log1p(jnp.exp(-jnp.abs(x)))


def _split3(x):
    hi = x.astype(BF16)
    r = x - hi.astype(F32)
    mid = r.astype(BF16)
    lo = (r - mid.astype(F32)).astype(BF16)
    return hi, mid, lo


def _window_cast_kernel(a_ref, h_ref, o_ref, *, shift):
    a = a_ref[...]
    if shift:
        a = jnp.concatenate([a[:, shift:], h_ref[:, :shift]], axis=1)
    o_ref[...] = a.astype(BF16)


def _window_bf16(w, col0, ncols):
    depth, d, _ = w.shape
    shift = col0 % LANES
    base = col0 - shift
    tc = _pick_tile(math.gcd(base, ncols) if base else ncols, 1024, LANES)
    tr = _pick_tile(d, 512, 8)
    return pl.pallas_call(
        functools.partial(_window_cast_kernel, shift=shift),
        grid=(depth, d // tr, ncols // tc),
        in_specs=[
            pl.BlockSpec((None, tr, tc), lambda l, r, c: (l, r, c + base // tc)),
            pl.BlockSpec((None, tr, LANES), lambda l, r, c: (l, r, (base + (c + 1) * tc) // LANES)),
        ],
        out_specs=pl.BlockSpec((None, tr, tc), lambda l, r, c: (l, r, c)),
        out_shape=jax.ShapeDtypeStruct((depth, d, ncols), BF16),
        compiler_params=_params(("parallel", "parallel", "parallel")),
        name="window_cast",
    )(w, w)


def _ffn_kernel(x_ref, wg_ref, wu_ref, wd_ref, g_ref, b_ref, o_ref, xb_ref, *, alpha, nf):
    f = pl.program_id(1)

    @pl.when(f == 0)
    def _():
        xb_ref[...] = x_ref[...].astype(BF16)
        o_ref[...] = jnp.zeros_like(o_ref)

    xb = xb_ref[...]
    hg = jnp.dot(xb, wg_ref[...], preferred_element_type=F32)
    hu = jnp.dot(xb, wu_ref[...], preferred_element_type=F32)
    a = (_silu(hg) * hu).astype(BF16)
    o_ref[...] += jnp.dot(a, wd_ref[...], preferred_element_type=F32)

    @pl.when(f == nf - 1)
    def _():
        o_ref[...] = _layer_norm(alpha * x_ref[...] + 0.5 * o_ref[...], g_ref[...], b_ref[...])


def _ffn(x, w_gu, w_down, ln_g, ln_b, l, ln_idx, *, alpha, tm):
    mt, d = x.shape
    dff = w_down.shape[1]
    tf = _pick_tile(dff, 512, LANES)
    nf = dff // tf
    return pl.pallas_call(
        functools.partial(_ffn_kernel, alpha=alpha, nf=nf),
        grid=(mt // tm, nf),
        in_specs=[
            pl.BlockSpec((tm, d), lambda m, f: (m, 0), pipeline_mode=pl.Buffered(1)),
            pl.BlockSpec((None, d, tf), lambda m, f: (l, 0, f)),
            pl.BlockSpec((None, d, tf), lambda m, f: (l, 0, f + nf)),
            pl.BlockSpec((None, tf, d), lambda m, f: (l, f, 0)),
            pl.BlockSpec((None, 1, d), lambda m, f: (ln_idx, 0, 0)),
            pl.BlockSpec((None, 1, d), lambda m, f: (ln_idx, 0, 0)),
        ],
        out_specs=pl.BlockSpec((tm, d), lambda m, f: (m, 0)),
        out_shape=jax.ShapeDtypeStruct((mt, d), F32),
        scratch_shapes=[pltpu.VMEM((tm, d), BF16)],
        compiler_params=_params(("parallel", "arbitrary")),
        name="ffn",
    )(x, w_gu, w_gu, w_down, ln_g, ln_b)


def _proj_a_kernel(x_ref, w_ref, wf_ref, bf_ref, z_ref, lf_ref, xb_ref, kv_ref):
    n = pl.program_id(1)

    @pl.when(n == 0)
    def _():
        xb0 = x_ref[...].astype(BF16)
        xb_ref[...] = xb0
        fa = jnp.dot(xb0, wf_ref[...], preferred_element_type=F32) + bf_ref[...]
        lf_ref[...] = _log_sigmoid(fa)

    z = jnp.dot(xb_ref[...], w_ref[...], preferred_element_type=F32)
    z_ref[...] = z

    @pl.when(n >= 1)
    def _():
        kv_ref[...] = z.astype(BF16)


def _proj_a(x, w16, w_f, b_f, l, *, wa, tm):
    mt, d = x.shape
    return pl.pallas_call(
        _proj_a_kernel,
        grid=(mt // tm, 3),
        in_specs=[
            pl.BlockSpec((tm, d), lambda m, n: (m, 0)),
            pl.BlockSpec((None, d, wa), lambda m, n: (l, 0, n)),
            pl.BlockSpec((None, d, LANES), lambda m, n: (l, 0, 0)),
            pl.BlockSpec((None, 1, LANES), lambda m, n: (l, 0, 0)),
        ],
        out_specs=[
            pl.BlockSpec((tm, wa), lambda m, n: (m, n)),
            pl.BlockSpec((tm, LANES), lambda m, n: (m, 0)),
            pl.BlockSpec((tm, d), lambda m, n: (m, 0)),
            pl.BlockSpec((tm, wa), lambda m, n: (m, jnp.maximum(n - 1, 0))),
        ],
        out_shape=[jax.ShapeDtypeStruct((mt, 3 * wa), F32), jax.ShapeDtypeStruct((mt, LANES), F32),
                   jax.ShapeDtypeStruct((mt, d), BF16), jax.ShapeDtypeStruct((mt, 2 * wa), BF16)],
        compiler_params=_params(("parallel", "arbitrary")),
        name="proj_a",
    )(x, w16, w_f, b_f)


def _proj_b_kernel(xb_ref, w_ref, cos_ref, sin_ref, z_ref, *, n_v, dk, kscale):
    n = pl.program_id(1)
    acc = jnp.dot(xb_ref[...], w_ref[...], preferred_element_type=F32)
    n_heads2 = acc.shape[1] // dk

    @pl.when(n == 0)
    def _():
        cos = cos_ref[...]
        sin = sin_ref[...]
        for h in range(n_heads2):
            xh = acc[:, h * dk:(h + 1) * dk]
            r = xh * cos + pltpu.roll(xh, dk // 2, 1) * sin
            if h >= n_heads2 // 2:
                r = r * kscale
            z_ref[:, h * dk:(h + 1) * dk] = r.astype(BF16)

    @pl.when(jnp.logical_and(n >= 1, n < 1 + n_v))
    def _():
        z_ref[...] = acc.astype(BF16)

    @pl.when(n >= 1 + n_v)
    def _():
        z_ref[...] = _silu(acc).astype(BF16)


def _proj_b(xb, w_main, cos_t, sin_t, l, *, col0, wqk, wv, dk, tm):
    mt, d = xb.shape
    assert dk == LANES
    tn = 2 * wqk
    assert wv % tn == 0 and col0 % tn == 0
    ncols = 2 * wqk + 2 * wv
    kern = functools.partial(_proj_b_kernel, n_v=wv // tn, dk=dk, kscale=dk ** -0.5)
    return pl.pallas_call(
        kern,
        grid=(mt // tm, ncols // tn),
        in_specs=[
            pl.BlockSpec((tm, d), lambda m, n: (m, 0)),
            pl.BlockSpec((None, d, tn), lambda m, n: (l, 0, n + col0 // tn)),
            pl.BlockSpec((tm, dk), lambda m, n: (m, 0)),
            pl.BlockSpec((tm, dk), lambda m, n: (m, 0)),
        ],
        out_specs=pl.BlockSpec((tm, tn), lambda m, n: (m, n)),
        out_shape=jax.ShapeDtypeStruct((mt, ncols), BF16),
        compiler_params=_params(("parallel", "arbitrary")),
        name="proj_b",
    )(xb, w_main, cos_t, sin_t)


def _proj_c_kernel(xb_ref, w_ref, g_ref, b_ref, u_ref, vn_ref):
    n = pl.program_id(1)
    acc = _gelu_tanh(jnp.dot(xb_ref[...], w_ref[...], preferred_element_type=F32))

    @pl.when(n == 0)
    def _():
        u_ref[...] = acc.astype(BF16)

    @pl.when(n == 1)
    def _():
        vn_ref[...] = _layer_norm(acc, g_ref[...], b_ref[...])


def _proj_c(xb, w_main, vln_g, vln_b, l, *, col0, wc, tm):
    mt, d = xb.shape
    assert col0 % wc == 0
    return pl.pallas_call(
        _proj_c_kernel,
        grid=(mt // tm, 2),
        in_specs=[
            pl.BlockSpec((tm, d), lambda m, n: (m, 0)),
            pl.BlockSpec((None, d, wc), lambda m, n: (l, 0, n + col0 // wc)),
            pl.BlockSpec((None, 1, wc), lambda m, n: (l, 0, 0)),
            pl.BlockSpec((None, 1, wc), lambda m, n: (l, 0, 0)),
        ],
        out_specs=[
            pl.BlockSpec((tm, wc), lambda m, n: (m, 0)),
            pl.BlockSpec((tm, wc), lambda m, n: (m, 0)),
        ],
        out_shape=[jax.ShapeDtypeStruct((mt, wc), BF16), jax.ShapeDtypeStruct((mt, wc), F32)],
        compiler_params=_params(("parallel", "arbitrary")),
        name="proj_c",
    )(xb, w_main, vln_g, vln_b)


def _proj_gate_kernel(xb_ref, w_ref, z_ref):
    acc = jnp.dot(xb_ref[...], w_ref[...], preferred_element_type=F32)
    z_ref[...] = jax.nn.sigmoid(acc).astype(BF16)


def _proj_gate(xb, w_main, l, *, col0, ncols, tm):
    mt, d = xb.shape
    tn = _pick_tile(math.gcd(ncols, col0), PROJ_TN, LANES)
    return pl.pallas_call(
        _proj_gate_kernel,
        grid=(mt // tm, ncols // tn),
        in_specs=[
            pl.BlockSpec((tm, d), lambda m, n: (m, 0)),
            pl.BlockSpec((None, d, tn), lambda m, n: (l, 0, n + col0 // tn)),
        ],
        out_specs=pl.BlockSpec((tm, tn), lambda m, n: (m, n)),
        out_shape=jax.ShapeDtypeStruct((mt, ncols), BF16),
        compiler_params=_params(("parallel", "arbitrary")),
        name="proj_gate",
    )(xb, w_main)


def _rope_kernel(pos_ref, cos_ref, sin_ref, *, half):
    lane = lax.broadcasted_iota(jnp.int32, pos_ref.shape, 1)
    j = jnp.where(lane >= half, lane - half, lane).astype(F32)
    inv = jnp.exp(j * (-math.log(ROPE_BASE) / half))
    ang = pos_ref[...] * inv
    cos_ref[...] = jnp.cos(ang)
    sin_ref[...] = jnp.where(lane >= half, 1.0, -1.0) * jnp.sin(ang)


def _rope_tables(pos_b, *, tm):
    mt, dk = pos_b.shape
    spec = pl.BlockSpec((tm, dk), lambda m: (m, 0))
    return pl.pallas_call(
        functools.partial(_rope_kernel, half=dk // 2),
        grid=(mt // tm,),
        in_specs=[spec],
        out_specs=[spec, spec],
        out_shape=[jax.ShapeDtypeStruct((mt, dk), F32)] * 2,
        compiler_params=_params(("parallel",)),
        name="rope_tables",
    )(pos_b)


def _cumsum_kernel(lf_ref, c_ref, carry_ref):
    @pl.when(pl.program_id(1) == 0)
    def _():
        carry_ref[...] = jnp.zeros_like(carry_ref)

    x = lf_ref[...]
    tc = x.shape[0]
    row = lax.broadcasted_iota(jnp.int32, (tc, tc), 0)
    col = lax.broadcasted_iota(jnp.int32, (tc, tc), 1)
    tri = jnp.where(col <= row, 1.0, 0.0).astype(BF16)
    c = carry_ref[0:1, :]
    for part in _split3(x):
        c = c + jnp.dot(tri, part, preferred_element_type=F32)
    c_ref[...] = c
    carry_ref[...] = jnp.broadcast_to(c[tc - 1:tc, :], carry_ref.shape)


def _cumsum(lf, *, batch, seq):
    tc = _pick_tile(seq, 512, 8)
    nt = seq // tc
    return pl.pallas_call(
        _cumsum_kernel,
        grid=(batch, nt),
        in_specs=[pl.BlockSpec((tc, LANES), lambda b, t: (b * nt + t, 0))],
        out_specs=pl.BlockSpec((tc, LANES), lambda b, t: (b * nt + t, 0)),
        out_shape=jax.ShapeDtypeStruct((batch * seq, LANES), F32),
        scratch_shapes=[pltpu.VMEM((8, LANES), F32)],
        compiler_params=_params(("parallel", "arbitrary")),
        name="logf_cumsum",
    )(lf)


def _fox_prompt_kernel(base_ref, q_ref, kb_ref, vb_ref, ck_ref, o_ref, m_ref, l_ref, acc_ref,
                       *, tq, hps, hd, scale):
    del base_ref
    i = pl.program_id(2)
    log2e = math.log2(math.e)

    qs = [(q_ref[:, hh * hd:(hh + 1) * hd] * (scale * log2e)).astype(BF16) for hh in range(hps)]
    m_ref[...] = jnp.full_like(m_ref, NEG_BIG)
    l_ref[...] = jnp.zeros_like(l_ref)
    acc_ref[...] = jnp.zeros_like(acc_ref)

    def step(j, masked):
        off = pl.multiple_of(j * tq, tq)
        for hh in range(hps):
            k = kb_ref[pl.ds(off, tq), hh * hd:(hh + 1) * hd]
            v = vb_ref[pl.ds(off, tq), hh * hd:(hh + 1) * hd]
            s = lax.dot_general(qs[hh], k, (((1,), (1,)), ((), ())), preferred_element_type=F32)
            s = s - ck_ref[hh, j] * log2e
            if masked:
                row = lax.broadcasted_iota(jnp.int32, s.shape, 0)
                col = lax.broadcasted_iota(jnp.int32, s.shape, 1)
                s = jnp.where(col <= row, s, NEG_BIG)
            m_prev = m_ref[hh]
            m_new = jnp.maximum(m_prev, jnp.max(s, axis=1, keepdims=True))
            p = jnp.exp2(s - jnp.tile(m_new, (1, tq // LANES)))
            alpha = jnp.exp2(m_prev - m_new)
            l_ref[hh] = alpha * l_ref[hh] + jnp.sum(p, axis=1, keepdims=True)
            acc_ref[hh] = alpha * acc_ref[hh] + jnp.dot(p.astype(BF16), v, preferred_element_type=F32)
            m_ref[hh] = m_new

    def body(j, carry):
        step(j, False)
        return carry

    lax.fori_loop(0, i, body, 0)
    step(i, True)
    for hh in range(hps):
        o_ref[:, hh * hd:(hh + 1) * hd] = (acc_ref[hh] / l_ref[hh]).astype(BF16)


def _fox_prompt(base, z_a, kv16, ck, *, batch, seq, heads, hd):
    assert hd == LANES
    tq = _pick_tile(seq, 512, LANES)
    nq = seq // tq
    hps = _pick_tile(heads, 4, 1)
    wb = hps * hd
    ng = heads // hps
    kern = functools.partial(_fox_prompt_kernel, tq=tq, hps=hps, hd=hd, scale=hd ** -0.5)
    return pl.pallas_call(
        kern,
        grid=(batch, ng, nq),
        in_specs=[
            pl.BlockSpec(memory_space=pl.ANY),
            pl.BlockSpec((tq, wb), lambda b, h, i: (b * nq + i, h)),
            pl.BlockSpec((seq, wb), lambda b, h, i: (b, h)),
            pl.BlockSpec((seq, wb), lambda b, h, i: (b, ng + h)),
            pl.BlockSpec((hps, nq, 1, tq), lambda b, h, i: (b * ng + h, 0, 0, 0)),
        ],
        out_specs=pl.BlockSpec((tq, wb), lambda b, h, i: (b * nq + i, h)),
        out_shape=jax.ShapeDtypeStruct(base.shape, base.dtype),
        input_output_aliases={0: 0},
        scratch_shapes=[
            pltpu.VMEM((hps, tq, LANES), F32),
            pltpu.VMEM((hps, tq, LANES), F32),
            pltpu.VMEM((hps, tq, hd), F32),
        ],
        compiler_params=_params(("parallel", "parallel", "arbitrary")),
        name="fox_prompt",
    )(base, z_a, kv16, kv16, ck)


def _fox_sample_kernel(pt_ref, qkv_ref, lfn_ref, ck_hbm, cv_hbm, lf_hbm, o_ref,
                       kbuf, vbuf, lbuf, sem, qs_ref, m_ref, l_ref, acc_ref, carry_ref,
                       *, layer, heads, hd, scale, n_pages, gp, dec_batch):
    b = pl.program_id(0)
    g = pl.program_id(1)
    ng = n_pages // gp
    step = b * ng + g
    slot = step % 2
    rows = lbuf.shape[2]

    def copies(bb, gg, sl):
        out = []
        for p in range(gp):
            pid = pt_ref[bb, n_pages - 1 - (gg * gp + p)]
            out.append(pltpu.make_async_copy(ck_hbm.at[layer, pid], kbuf.at[sl, p], sem.at[0, sl]))
            out.append(pltpu.make_async_copy(cv_hbm.at[layer, pid], vbuf.at[sl, p], sem.at[1, sl]))
            out.append(pltpu.make_async_copy(lf_hbm.at[layer, pid], lbuf.at[sl, p], sem.at[2, sl]))
        return out

    @pl.when(step == 0)
    def _():
        for c in copies(b, g, slot):
            c.start()

    @pl.when(step + 1 < dec_batch * ng)
    def _():
        wrap = g + 1 == ng
        for c in copies(jnp.where(wrap, b + 1, b), jnp.where(wrap, 0, g + 1), 1 - slot):
            c.start()

    lane = lax.broadcasted_iota(jnp.int32, (LANES, LANES), 1)
    src = lax.broadcasted_iota(jnp.int32, (LANES, LANES), 0)
    same_head = (src % heads) == (lane % heads)

    @pl.when(g == 0)
    def _():
        q = qkv_ref[0:heads, :] * scale
        qs_ref[...] = q.astype(BF16)
        m_ref[...] = jnp.sum(q * qkv_ref[heads:2 * heads, :], axis=1, keepdims=True)
        l_ref[...] = jnp.ones_like(l_ref)
        acc_ref[...] = qkv_ref[2 * heads:3 * heads, :]
        spread = jnp.where(src == lane % heads, 1.0, 0.0).astype(BF16)
        carry = jnp.zeros((1, LANES), F32)
        for part in _split3(lfn_ref[...]):
            carry = carry + jnp.dot(part, spread, preferred_element_type=F32)
        carry_ref[...] = carry

    for c in copies(b, g, slot):
        c.wait()

    later_w = jnp.where(jnp.logical_and(same_head, src > lane), 1.0, 0.0).astype(BF16)
    whole_w = jnp.where(same_head, 1.0, 0.0).astype(BF16)
    mix_w = jnp.concatenate([later_w, whole_w], axis=1)
    lf_all = lbuf[slot].reshape(gp * rows, LANES)
    sums = jnp.zeros((gp * rows, 2 * LANES), F32)
    for part in _split3(lf_all):
        sums = sums + jnp.dot(part, mix_w, preferred_element_type=F32)

    row_id = lax.broadcasted_iota(jnp.int32, (rows, LANES), 0)
    valid = (lax.broadcasted_iota(jnp.int32, (heads, LANES), 1) % heads
             == lax.broadcasted_iota(jnp.int32, (heads, LANES), 0))
    qs = qs_ref[...]
    carry = carry_ref[...]
    tiles = []
    for p in range(gp):
        within = sums[p * rows:(p + 1) * rows, :LANES]
        tot = sums[p * rows:(p + 1) * rows, LANES:]
        suf = tot
        sh = 1
        while sh < rows:
            suf = suf + jnp.where(row_id < rows - sh, pltpu.roll(suf, rows - sh, 0), 0.0)
            sh *= 2
        bias = carry + within + (suf - tot)
        carry = carry + suf[0:1, :]
        k2 = kbuf[slot, p].reshape(rows * LANES, hd).astype(BF16)
        s = lax.dot_general(qs, k2, (((1,), (1,)), ((), ())), preferred_element_type=F32)
        for r in range(rows):
            t = s[:, r * LANES:(r + 1) * LANES] + jnp.broadcast_to(bias[r:r + 1, :], (heads, LANES))
            tiles.append(jnp.where(valid, t, NEG_BIG))
    carry_ref[...] = carry

    m_tile = tiles[0]
    for t in tiles[1:]:
        m_tile = jnp.maximum(m_tile, t)
    m_prev = m_ref[...]
    m_new = jnp.maximum(m_prev, jnp.max(m_tile, axis=1, keepdims=True))
    alpha = jnp.exp(m_prev - m_new)
    probs = [jnp.exp(t - m_new) for t in tiles]
    p_sum = probs[0]
    for t in probs[1:]:
        p_sum = p_sum + t
    l_ref[...] = alpha * l_ref[...] + jnp.sum(p_sum, axis=1, keepdims=True)
    acc = alpha * acc_ref[...]
    for p in range(gp):
        pp = jnp.concatenate(probs[p * rows:(p + 1) * rows], axis=1).astype(BF16)
        v2 = vbuf[slot, p].reshape(rows * LANES, hd).astype(BF16)
        acc = acc + jnp.dot(pp, v2, preferred_element_type=F32)
    acc_ref[...] = acc
    m_ref[...] = m_new

    @pl.when(g == ng - 1)
    def _():
        o_ref[...] = acc_ref[...] / l_ref[...]


def _fox_sample(qkv_s, lf_s, cache_k, cache_v, cache_lf, page_table, l, *, heads, hd):
    dec_batch = qkv_s.shape[0]
    n_pages = page_table.shape[1]
    page = cache_k.shape[2]
    rows = cache_lf.shape[2]
    assert LANES % heads == 0 and rows * LANES == page * heads
    gp = _pick_tile(n_pages, 8, 1)
    kern = functools.partial(_fox_sample_kernel, layer=l, heads=heads, hd=hd, scale=hd ** -0.5,
                             n_pages=n_pages, gp=gp, dec_batch=dec_batch)
    grid_spec = pltpu.PrefetchScalarGridSpec(
        num_scalar_prefetch=1,
        grid=(dec_batch, n_pages // gp),
        in_specs=[
            pl.BlockSpec((None, 3 * heads, hd), lambda b, g, pt: (b, 0, 0)),
            pl.BlockSpec((None, 1, LANES), lambda b, g, pt: (b, 0, 0)),
            pl.BlockSpec(memory_space=pl.ANY),
            pl.BlockSpec(memory_space=pl.ANY),
            pl.BlockSpec(memory_space=pl.ANY),
        ],
        out_specs=pl.BlockSpec((None, heads, hd), lambda b, g, pt: (b, 0, 0)),
        scratch_shapes=[
            pltpu.VMEM((2, gp, page, heads, hd), F32),
            pltpu.VMEM((2, gp, page, heads, hd), F32),
            pltpu.VMEM((2, gp, rows, LANES), F32),
            pltpu.SemaphoreType.DMA((3, 2)),
            pltpu.VMEM((heads, hd), BF16),
            pltpu.VMEM((heads, 1), F32),
            pltpu.VMEM((heads, 1), F32),
            pltpu.VMEM((heads, hd), F32),
            pltpu.VMEM((1, LANES), F32),
        ],
    )
    return pl.pallas_call(
        kern,
        grid_spec=grid_spec,
        out_shape=jax.ShapeDtypeStruct((dec_batch, heads, hd), F32),
        compiler_params=_params(("arbitrary", "arbitrary")),
        name="fox_sample",
    )(page_table, qkv_s, lf_s, cache_k, cache_v, cache_lf)


def _log_gammas(n_heads):
    return [math.log1p(-(2.0 ** (-5.0 - h))) for h in range(n_heads)]


def _group_norm_gate(o, gate, g, b):
    mu = jnp.mean(o, axis=-1, keepdims=True)
    d = o - mu
    var = jnp.mean(d * d, axis=-1, keepdims=True)
    return gate * (d * lax.rsqrt(var + LN_EPS) * g + b)


def _ret_prompt_kernel(base_ref, q_ref, k_ref, v_ref, gate_ref, g_ref, b_ref, y_ref, st_ref,
                       *, heads, dk, dv, chunk):
    del base_ref

    @pl.when(pl.program_id(1) == 0)
    def _():
        st_ref[...] = jnp.zeros_like(st_ref)

    n_col = lax.broadcasted_iota(jnp.int32, (chunk, 1), 0).astype(F32)
    diff = (lax.broadcasted_iota(jnp.int32, (chunk, chunk), 0)
            - lax.broadcasted_iota(jnp.int32, (chunk, chunk), 1)).astype(F32)
    for h, lg in enumerate(_log_gammas(heads)):
        dmask = jnp.where(diff >= 0, jnp.exp(lg * jnp.maximum(diff, 0.0)), 0.0)
        dec_q = jnp.exp(lg * (n_col + 1.0))
        dec_k = jnp.exp(lg * (chunk - 1.0 - n_col))
        for c in range(q_ref.shape[0] // chunk):
            rows = slice(c * chunk, (c + 1) * chunk)
            q = q_ref[rows, h * dk:(h + 1) * dk]
            k = k_ref[rows, h * dk:(h + 1) * dk]
            v = v_ref[rows, h * dv:(h + 1) * dv]
            state = st_ref[h]
            a = lax.dot_general(q, k, (((1,), (1,)), ((), ())), preferred_element_type=F32) * dmask
            o = jnp.dot(a.astype(BF16), v, preferred_element_type=F32)
            o = o + jnp.dot(q, state.astype(BF16), preferred_element_type=F32) * dec_q
            kdec = k.astype(F32) * dec_k
            st_ref[h] = state * math.exp(lg * chunk) + jnp.dot(kdec.T.astype(BF16), v, preferred_element_type=F32)
            gate = gate_ref[rows, h * dv:(h + 1) * dv].astype(F32)
            y = _group_norm_gate(o, gate, g_ref[h:h + 1, :], b_ref[h:h + 1, :])
            y_ref[rows, h * dv:(h + 1) * dv] = y.astype(BF16)


def _ret_prompt(base, z_b, gn_g, gn_b, l, *, batch, seq, heads, dk, dv):
    chunk = LANES
    tr = _pick_tile(seq, 4 * chunk, chunk)
    nc = seq // tr
    wqk, wv = heads * dk, heads * dv
    assert wv % wqk == 0
    kern = functools.partial(_ret_prompt_kernel, heads=heads, dk=dk, dv=dv, chunk=chunk)
    return pl.pallas_call(
        kern,
        grid=(batch, nc),
        in_specs=[
            pl.BlockSpec(memory_space=pl.ANY),
            pl.BlockSpec((tr, wqk), lambda b, c: (b * nc + c, 0)),
            pl.BlockSpec((tr, wqk), lambda b, c: (b * nc + c, 1)),
            pl.BlockSpec((tr, wv), lambda b, c: (b * nc + c, 2 * wqk // wv)),
            pl.BlockSpec((tr, wv), lambda b, c: (b * nc + c, 2 * wqk // wv + 1)),
            pl.BlockSpec((None, heads, dv), lambda b, c: (l, 0, 0)),
            pl.BlockSpec((None, heads, dv), lambda b, c: (l, 0, 0)),
        ],
        out_specs=[
            pl.BlockSpec((tr, wv), lambda b, c: (b * nc + c, 0)),
            pl.BlockSpec((None, heads, dk, dv), lambda b, c: (b, 0, 0, 0)),
        ],
        out_shape=[
            jax.ShapeDtypeStruct(base.shape, base.dtype),
            jax.ShapeDtypeStruct((batch, heads, dk, dv), F32),
        ],
        input_output_aliases={0: 0},
        compiler_params=_params(("parallel", "arbitrary")),
        name="ret_prompt",
    )(base, z_b, z_b, z_b, z_b, gn_g, gn_b)


def _sample_mix_kernel(zb_ref, st_ref, g_ref, b_ref, u_ref, vn_ref, ws_ref, bs_ref, y_ref, yc_ref, nst_ref,
                       *, heads, dk, dv):
    b = pl.program_id(0)

    @pl.when(b == 0)
    def _():
        y_ref[...] = jnp.zeros_like(y_ref)
        yc_ref[...] = u_ref[...].astype(F32) * (ws_ref[...] * vn_ref[...] + bs_ref[...])

    wqk, wv = heads * dk, heads * dv
    blk = zb_ref[...].astype(F32)
    pick = lax.broadcasted_iota(jnp.int32, blk.shape, 0) == b
    row = jnp.sum(jnp.where(pick, blk, 0.0), axis=0, keepdims=True)
    eye =(lax.broadcasted_iota(jnp.int32, (dk, dk), 0) == lax.broadcasted_iota(jnp.int32, (dk, dk), 1))
    for h, lg in enumerate(_log_gammas(heads)):
        q = row[:, h * dk:(h + 1) * dk]
        k = row[:, wqk + h * dk:wqk + (h + 1) * dk]
        v = row[:, 2 * wqk + h * dv:2 * wqk + (h + 1) * dv]
        gate = row[:, 2 * wqk + wv + h * dv:2 * wqk + wv + (h + 1) * dv]
        qcol = jnp.sum(jnp.where(eye, jnp.broadcast_to(q, (dk, dk)), 0.0), axis=1, keepdims=True)
        kcol = jnp.sum(jnp.where(eye, jnp.broadcast_to(k, (dk, dk)), 0.0), axis=1, keepdims=True)
        state = st_ref[h]
        gamma = math.exp(lg)
        new_state = state * gamma + kcol * v
        nst_ref[h] = new_state
        o = jnp.sum(q * k, axis=1, keepdims=True) * v + jnp.sum(qcol * state, axis=0, keepdims=True) * gamma
        y = _group_norm_gate(o, gate, g_ref[h:h + 1, :], b_ref[h:h + 1, :])
        y_ref[pl.ds(b, 1), h * dv:(h + 1) * dv] = y


def _sample_mix(z_b, state_ret, gn_g, gn_b, u_c, vn, ws_row, bs_row, l, *, row0, dec_batch, heads, dk, dv):
    wv = heads * dv
    wc = u_c.shape[1]
    rb = row0 // SAMPLE_ROWS
    kern = functools.partial(_sample_mix_kernel, heads=heads, dk=dk, dv=dv)
    return pl.pallas_call(
        kern,
        grid=(dec_batch,),
        in_specs=[
            pl.BlockSpec((SAMPLE_ROWS, z_b.shape[1]), lambda b: (rb, 0)),
            pl.BlockSpec((None, None, heads, dk, dv), lambda b: (l, b, 0, 0, 0)),
            pl.BlockSpec((None, heads, dv), lambda b: (l, 0, 0)),
            pl.BlockSpec((None, heads, dv), lambda b: (l, 0, 0)),
            pl.BlockSpec((SAMPLE_ROWS, wc), lambda b: (rb, 0)),
            pl.BlockSpec((SAMPLE_ROWS, wc), lambda b: (rb, 0)),
            pl.BlockSpec((None, 1, wc), lambda b: (l, 0, 0)),
            pl.BlockSpec((None, 1, wc), lambda b: (l, 0, 0)),
        ],
        out_specs=[
            pl.BlockSpec((SAMPLE_ROWS, wv), lambda b: (0, 0)),
            pl.BlockSpec((SAMPLE_ROWS, wc), lambda b: (0, 0)),
            pl.BlockSpec((None, heads, dk, dv), lambda b: (b, 0, 0, 0)),
        ],
        out_shape=[
            jax.ShapeDtypeStruct((SAMPLE_ROWS, wv), F32),
            jax.ShapeDtypeStruct((SAMPLE_ROWS, wc), F32),
            jax.ShapeDtypeStruct((dec_batch, heads, dk, dv), F32),
        ],
        compiler_params=_params(("arbitrary",)),
        name="sample_mix",
    )(z_b, state_ret, gn_g, gn_b, u_c, vn, ws_row, bs_row)


def _chunk_mix_kernel(base_ref, vn_ref, u_ref, ws_ref, bst_ref, y_ref, *, groups, cg):
    del base_ref
    chunk = ws_ref.shape[1]
    lower = (lax.broadcasted_iota(jnp.int32, (chunk, chunk), 1)
             <= lax.broadcasted_iota(jnp.int32, (chunk, chunk), 0))
    bst = bst_ref[...]
    for g in range(groups):
        w = jnp.where(lower, ws_ref[g], 0.0).astype(BF16)
        cols = slice(g * cg, (g + 1) * cg)
        for c in range(vn_ref.shape[0] // chunk):
            rows = slice(c * chunk, (c + 1) * chunk)
            s = jnp.dot(w, vn_ref[rows, cols].astype(BF16), preferred_element_type=F32) + bst[:, g:g + 1]
            y_ref[rows, cols] = (u_ref[rows, cols].astype(F32) * s).astype(BF16)


def _chunk_mix(base, vn, u_c, w_spatial, bs_t, l, *, rows, groups, cg):
    chunk = w_spatial.shape[2]
    wc = groups * cg
    tr = _pick_tile(rows, 4 * chunk, chunk)
    kern = functools.partial(_chunk_mix_kernel, groups=groups, cg=cg)
    return pl.pallas_call(
        kern,
        grid=(rows // tr,),
        in_specs=[
            pl.BlockSpec(memory_space=pl.ANY),
            pl.BlockSpec((tr, wc), lambda n: (n, 0)),
            pl.BlockSpec((tr, wc), lambda n: (n, 0)),
            pl.BlockSpec((None, groups, chunk, chunk), lambda n: (l, 0, 0, 0)),
            pl.BlockSpec((None, chunk, groups), lambda n: (l, 0, 0)),
        ],
        out_specs=pl.BlockSpec((tr, wc), lambda n: (n, 0)),
        out_shape=jax.ShapeDtypeStruct(base.shape, base.dtype),
        input_output_aliases={0: 0},
        compiler_params=_params(("parallel",)),
        name="chunk_mix",
    )(base, vn, u_c, w_spatial, bs_t)


def _merge_kernel(oa_ref, yb_ref, yc_ref, ga_ref, gb_ref, gc_ref, wa_ref, wb_ref, wc_ref, y_ref):
    ya = jnp.dot(oa_ref[...], wa_ref[...], preferred_element_type=F32)
    yb = jnp.dot(yb_ref[...], wb_ref[...], preferred_element_type=F32)
    yc = jnp.dot(yc_ref[...], wc_ref[...], preferred_element_type=F32)
    y = ga_ref[...].astype(F32) * ya + gb_ref[...].astype(F32) * yb + gc_ref[...].astype(F32) * yc
    y_ref[...] = y.astype(BF16)


def _merge(o_a, y_b, y_c, gates, w_up_a, w_up_b, w_up_c, l, *, tm):
    mt = o_a.shape[0]
    d = w_up_a.shape[2]
    tn = _pick_tile(d, PROJ_TN, LANES)
    nn = d // tn

    def act(a):
        return pl.BlockSpec((tm, a.shape[1]), lambda m, n: (m, 0))

    def wgt(w):
        return pl.BlockSpec((None, w.shape[1], tn), lambda m, n: (l, 0, n))

    return pl.pallas_call(
        _merge_kernel,
        grid=(mt // tm, nn),
        in_specs=[
            act(o_a), act(y_b), act(y_c),
            pl.BlockSpec((tm, tn), lambda m, n: (m, n)),
            pl.BlockSpec((tm, tn), lambda m, n: (m, n + nn)),
            pl.BlockSpec((tm, tn), lambda m, n: (m, n + 2 * nn)),
            wgt(w_up_a), wgt(w_up_b), wgt(w_up_c),
        ],
        out_specs=pl.BlockSpec((tm, tn), lambda m, n: (m, n)),
        out_shape=jax.ShapeDtypeStruct((mt, d), BF16),
        compiler_params=_params(("parallel", "arbitrary")),
        name="merge",
    )(o_a, y_b, y_c, gates, gates, gates, w_up_a, w_up_b, w_up_c)


def _out_kernel(x_ref, y_ref, w_ref, g_ref, b_ref, o_ref, *, alpha):
    sub = jnp.dot(y_ref[...], w_ref[...], preferred_element_type=F32)
    o_ref[...] = _layer_norm(alpha * x_ref[...] + sub, g_ref[...], b_ref[...])


def _out_proj(x, y, w_out, ln_g, ln_b, l, ln_idx, *, alpha):
    mt, d = x.shape
    tm = _pick_tile(mt, 512, SAMPLE_ROWS)
    row = pl.BlockSpec((tm, d), lambda m: (m, 0))
    return pl.pallas_call(
        functools.partial(_out_kernel, alpha=alpha),
        grid=(mt // tm,),
        in_specs=[
            row, row,
            pl.BlockSpec((None, d, d), lambda m: (l, 0, 0)),
            pl.BlockSpec((None, 1, d), lambda m: (ln_idx, 0, 0)),
            pl.BlockSpec((None, 1, d), lambda m: (ln_idx, 0, 0)),
        ],
        out_specs=row,
        out_shape=jax.ShapeDtypeStruct((mt, d), F32),
        compiler_params=_params(("parallel",)),
        name="out_proj",
    )(x, y, w_out, ln_g, ln_b)


def _sample_rows_base(sample_rows, n_prompt):
    pad = SAMPLE_ROWS - sample_rows.shape[0]
    return jnp.pad(sample_rows.astype(BF16), ((n_prompt, pad), (0, 0)))


def kernel(x_prompt, x_sample, cache_k, cache_v, cache_logf, state_ret, page_table, w_in, b_forget, w_up_a, w_up_b, w_up_c, w_out, ret_gn_g, ret_gn_b, vln_g, vln_b, w_spatial, b_spatial, ffa_gu, ffa_down, ffb_gu, ffb_down, ln_g, ln_b):
    batch, seq, d = x_prompt.shape
    dec_batch, dec_seq, _ = x_sample.shape
    depth = w_in.shape[0]
    heads_a, hd = cache_k.shape[3], cache_k.shape[4]
    n_pool, page = cache_k.shape[1], cache_k.shape[2]
    heads_b, dk, dv = state_ret.shape[2], state_ret.shape[3], state_ret.shape[4]
    groups, chunk = w_spatial.shape[1], w_spatial.shape[2]
    wa, wqk, wv, wc = heads_a * hd, heads_b * dk, heads_b * dv, vln_g.shape[1]
    cg = wc // groups
    assert dec_seq == 1 and dec_batch <= SAMPLE_ROWS and heads_a <= LANES
    assert seq % chunk == 0 and chunk == LANES and (batch * seq) % SAMPLE_ROWS == 0
    assert w_in.shape[2] == 3 * wa + heads_a + 2 * wqk + 2 * wv + 2 * wc + 3 * d
    past_len = page_table.shape[1] * page
    alpha = (2 * depth) ** 0.25

    n_prompt = batch * seq
    mt = n_prompt + SAMPLE_ROWS
    tm = _pick_tile(mt, 1024, SAMPLE_ROWS)

    w_abc = _window_bf16(w_in, 0, 3 * wa)
    w_rest = _window_bf16(w_in, 3 * wa + heads_a, 2 * wqk + 2 * wv + 2 * wc + 3 * d)
    w_f =jnp.pad(w_in[:, :, 3 * wa:3 * wa + heads_a], ((0, 0), (0, 0), (0, LANES - heads_a))).astype(BF16)
    b_f = jnp.pad(b_forget, ((0, 0), (0, LANES - heads_a)))[:, None, :]
    ffa_gu16, ffa_down16 = ffa_gu.astype(BF16), ffa_down.astype(BF16)
    ffb_gu16, ffb_down16 = ffb_gu.astype(BF16), ffb_down.astype(BF16)
    w_up_a16, w_up_b16, w_up_c16 = w_up_a.astype(BF16), w_up_b.astype(BF16), w_up_c.astype(BF16)
    w_out16 = w_out.astype(BF16)
    ln_g3 = ln_g.reshape(depth * 3, 1, d)
    ln_b3 = ln_b.reshape(depth * 3, 1, d)
    vln_g3, vln_b3 = vln_g[:, None, :], vln_b[:, None, :]
    bs_t = jnp.swapaxes(b_spatial, 1, 2)
    ws_row = jnp.repeat(w_spatial[:, :, 0, 0], cg, axis=1)[:, None, :]
    bs_row = jnp.repeat(b_spatial[:, :, 0], cg, axis=1)[:, None, :]
    cache_lf = cache_logf.reshape(depth, n_pool, page * heads_a // LANES, LANES)

    pos = jnp.concatenate([jnp.tile(jnp.arange(seq, dtype=F32), batch), jnp.full((SAMPLE_ROWS,), past_len, F32)])
    cos_t, sin_t = _rope_tables(jnp.broadcast_to(pos[:, None], (mt, dk)), tm=tm)

    x = jnp.concatenate([x_prompt.reshape(n_prompt, d), x_sample.reshape(dec_batch, d),
                         jnp.zeros((SAMPLE_ROWS - dec_batch, d), F32)], axis=0)

    col_c = 2 * wqk + 2 * wv
    col_g = col_c + 2 * wc
    s0, s1 = n_prompt, n_prompt + dec_batch
    outs = [[] for _ in range(9)]
    for l in range(depth):
        x = _ffn(x, ffa_gu16, ffa_down16, ln_g3, ln_b3, l, 3 * l, alpha=alpha, tm=tm)

        z_a, lf, xb, kv16 = _proj_a(x, w_abc, w_f, b_f, l, wa=wa, tm=tm)
        z_b = _proj_b(xb, w_rest, cos_t, sin_t, l, col0=0, wqk=wqk, wv=wv, dk=dk, tm=tm)
        u_c, vn = _proj_c(xb, w_rest, vln_g3, vln_b3, l, col0=col_c, wc=wc, tm=tm)
        gates = _proj_gate(xb, w_rest, l, col0=col_g, ncols=3 * d, tm=tm)

        c = _cumsum(lf, batch=batch, seq=seq)
        tq = _pick_tile(seq, 512, LANES)
        ck = jnp.swapaxes(c[:, :heads_a].reshape(batch, seq, heads_a), 1, 2).reshape(batch * heads_a, seq // tq, 1, tq)
        o_a_s = _fox_sample(z_a[s0:s1].reshape(dec_batch, 3 * heads_a, hd), lf[s0:s1, None, :],
                            cache_k, cache_v, cache_lf, page_table, l, heads=heads_a, hd=hd)
        y_b_s, y_c_s, st_s = _sample_mix(z_b, state_ret, ret_gn_g, ret_gn_b, u_c, vn, ws_row, bs_row, l,
                                         row0=n_prompt, dec_batch=dec_batch, heads=heads_b, dk=dk, dv=dv)

        o_a = _fox_prompt(_sample_rows_base(o_a_s.reshape(dec_batch, wa), n_prompt), z_a, kv16, ck,
                          batch=batch, seq=seq, heads=heads_a, hd=hd)
        y_b, st_p = _ret_prompt(_sample_rows_base(y_b_s[:dec_batch], n_prompt), z_b, ret_gn_g, ret_gn_b, l,
                                batch=batch, seq=seq, heads=heads_b, dk=dk, dv=dv)
        y_c = _chunk_mix(_sample_rows_base(y_c_s[:dec_batch], n_prompt), vn, u_c, w_spatial, bs_t, l,
                         rows=n_prompt, groups=groups, cg=cg)

        y = _merge(o_a, y_b, y_c, gates, w_up_a16, w_up_b16, w_up_c16, l, tm=tm)
        x = _out_proj(x, y, w_out16, ln_g3, ln_b3, l, 3 * l + 1, alpha=alpha)
        x = _ffn(x, ffb_gu16, ffb_down16, ln_g3, ln_b3, l, 3 * l + 2, alpha=alpha, tm=tm)

        outs[0].append(z_a[:n_prompt, wa:2 * wa].reshape(batch, seq, heads_a, hd))
        outs[1].append(z_a[:n_prompt, 2 * wa:].reshape(batch, seq, heads_a, hd))
        outs[2].append(lf[:n_prompt, :heads_a].reshape(batch, seq, heads_a))
        outs[3].append(st_p)
        outs[4].append(z_a[s0:s1, wa:2 * wa].reshape(dec_batch, 1, heads_a, hd))
        outs[5].append(z_a[s0:s1, 2 * wa:].reshape(dec_batch, 1, heads_a, hd))
        outs[6].append(lf[s0:s1, :heads_a].reshape(dec_batch, 1, heads_a))
        outs[7].append(st_s)
        outs[8].append(vn[s0:s1].reshape(dec_batch, 1, groups, cg))

    y_prompt = x[:n_prompt].reshape(batch, seq, d)
    y_sample = x[n_prompt:n_prompt + dec_batch].reshape(dec_batch, 1, d)
    return (y_prompt, y_sample) + tuple(jnp.stack(o) for o in outs)
```

```python
import functools
import math

import jax
import jax.numpy as jnp
from jax import lax
from jax.experimental import pallas as pl
from jax.experimental.pallas import tpu as pltpu

F32 = jnp.float32
BF16 = jnp.bfloat16

LN_EPS = 1e-5
ROPE_BASE = 10000.0
NEG_BIG = -1e30
LANES = 128
SAMPLE_ROWS = 16
PROJ_TN = 1024
V7X_VMEM_LIMIT = 56 * 1024 * 1024


def _pick_tile(n, max_tile, mult):
    best = None
    for t in range(mult, min(n, max_tile) + 1, mult):
        if n % t == 0:
            best = t
    assert best is not None, (n, max_tile, mult)
    return best


def _params(sem):
    return pltpu.CompilerParams(dimension_semantics=sem, vmem_limit_bytes=V7X_VMEM_LIMIT)


def _layer_norm(y, g, b):
    mu = jnp.mean(y, axis=-1, keepdims=True)
    d = y - mu
    var = jnp.mean(d * d, axis=-1, keepdims=True)
    return d * lax.rsqrt(var + LN_EPS) * g + b


def _silu(x):
    return x * jax.nn.sigmoid(x)


def _gelu_tanh(x):
    c = math.sqrt(2.0 / math.pi)
    return 0.5 * x * (1.0 + jnp.tanh(c * (x + 0.044715 * (x * x * x))))


def _log_sigmoid(x):
    return jnp.minimum(x, 0.0) - jnp.log1p(jnp.exp(-jnp.abs(x)))


def _split3(x):
    hi = x.astype(BF16)
    r = x - hi.astype(F32)
    mid = r.astype(BF16)
    lo = (r - mid.astype(F32)).astype(BF16)
    return hi, mid, lo


def _ffn_kernel(x_ref, wg_ref, wu_ref, wd_ref, g_ref, b_ref, o_ref, xb_ref, *, alpha, nf):
    f = pl.program_id(1)

    @pl.when(f == 0)
    def _():
        xb_ref[...] = x_ref[...].astype(BF16)
        o_ref[...] = jnp.zeros_like(o_ref)

    xb = xb_ref[...]
    hg = jnp.dot(xb, wg_ref[...], preferred_element_type=F32)
    hu = jnp.dot(xb, wu_ref[...], preferred_element_type=F32)
    a = (_silu(hg) * hu).astype(BF16)
    o_ref[...] += jnp.dot(a, wd_ref[...], preferred_element_type=F32)

    @pl.when(f == nf - 1)
    def _():
        o_ref[...] = _layer_norm(alpha * x_ref[...] + 0.5 * o_ref[...], g_ref[...], b_ref[...])


def _ffn(x, w_gu, w_down, ln_g, ln_b, l, ln_idx, *, alpha, tm):
    mt, d = x.shape
    dff = w_down.shape[1]
    tf = _pick_tile(dff, 512, LANES)
    nf = dff // tf
    return pl.pallas_call(
        functools.partial(_ffn_kernel, alpha=alpha, nf=nf),
        grid=(mt // tm, nf),
        in_specs=[
            pl.BlockSpec((tm, d), lambda m, f: (m, 0), pipeline_mode=pl.Buffered(1)),
            pl.BlockSpec((None, d, tf), lambda m, f: (l, 0, f)),
            pl.BlockSpec((None, d, tf), lambda m, f: (l, 0, f + nf)),
            pl.BlockSpec((None, tf, d), lambda m, f: (l, f, 0)),
            pl.BlockSpec((None, 1, d), lambda m, f: (ln_idx, 0, 0)),
            pl.BlockSpec((None, 1, d), lambda m, f: (ln_idx, 0, 0)),
        ],
        out_specs=pl.BlockSpec((tm, d), lambda m, f: (m, 0)),
        out_shape=jax.ShapeDtypeStruct((mt, d), F32),
        scratch_shapes=[pltpu.VMEM((tm, d), BF16)],
        compiler_params=_params(("parallel", "arbitrary")),
        name="ffn",
    )(x, w_gu, w_gu, w_down, ln_g, ln_b)


def _proj_a_kernel(x_ref, w_ref, wf_ref, bf_ref, z_ref, lf_ref, xb_ref, kv_ref):
    n = pl.program_id(1)

    @pl.when(n == 0)
    def _():
        xb0 = x_ref[...].astype(BF16)
        xb_ref[...] = xb0
        fa = jnp.dot(xb0, wf_ref[...], preferred_element_type=F32) + bf_ref[...]
        lf_ref[...] = _log_sigmoid(fa)

    z = jnp.dot(xb_ref[...], w_ref[...], preferred_element_type=F32)
    z_ref[...] = z

    @pl.when(n >= 1)
    def _():
        kv_ref[...] = z.astype(BF16)


def _proj_a(x, w16, w_f, b_f, l, *, wa, tm):
    mt, d = x.shape
    return pl.pallas_call(
        _proj_a_kernel,
        grid=(mt // tm, 3),
        in_specs=[
            pl.BlockSpec((tm, d), lambda m, n: (m, 0)),
            pl.BlockSpec((None, d, wa), lambda m, n: (l, 0, n)),
            pl.BlockSpec((None, d, LANES), lambda m, n: (l, 0, 0)),
            pl.BlockSpec((None, 1, LANES), lambda m, n: (l, 0, 0)),
        ],
        out_specs=[
            pl.BlockSpec((tm, wa), lambda m, n: (m, n)),
            pl.BlockSpec((tm, LANES), lambda m, n: (m, 0)),
            pl.BlockSpec((tm, d), lambda m, n: (m, 0)),
            pl.BlockSpec((tm, wa), lambda m, n: (m, jnp.maximum(n - 1, 0))),
        ],
        out_shape=[jax.ShapeDtypeStruct((mt, 3 * wa), F32), jax.ShapeDtypeStruct((mt, LANES), F32),
                   jax.ShapeDtypeStruct((mt, d), BF16), jax.ShapeDtypeStruct((mt, 2 * wa), BF16)],
        compiler_params=_params(("parallel", "arbitrary")),
        name="proj_a",
    )(x, w16, w_f, b_f)


def _proj_b_kernel(xb_ref, w_ref, cos_ref, sin_ref, z_ref, *, n_v, dk, kscale):
    n = pl.program_id(1)
    acc = jnp.dot(xb_ref[...], w_ref[...], preferred_element_type=F32)
    n_heads2 = acc.shape[1] // dk

    @pl.when(n == 0)
    def _():
        cos = cos_ref[...]
        sin = sin_ref[...]
        for h in range(n_heads2):
            xh = acc[:, h * dk:(h + 1) * dk]
            r = xh * cos + pltpu.roll(xh, dk // 2, 1) * sin
            if h >= n_heads2 // 2:
                r = r * kscale
            z_ref[:, h * dk:(h + 1) * dk] = r.astype(BF16)

    @pl.when(jnp.logical_and(n >= 1, n < 1 + n_v))
    def _():
        z_ref[...] = acc.astype(BF16)

    @pl.when(n >= 1 + n_v)
    def _():
        z_ref[...] = _silu(acc).astype(BF16)


def _proj_b(xb, w_main, cos_t, sin_t, l, *, col0, wqk, wv, dk, tm):
    mt, d = xb.shape
    assert dk == LANES
    tn = 2 * wqk
    assert wv % tn == 0 and col0 % tn == 0
    ncols = 2 * wqk + 2 * wv
    kern = functools.partial(_proj_b_kernel, n_v=wv // tn, dk=dk, kscale=dk ** -0.5)
    return pl.pallas_call(
        kern,
        grid=(mt // tm, ncols // tn),
        in_specs=[
            pl.BlockSpec((tm, d), lambda m, n: (m, 0)),
            pl.BlockSpec((None, d, tn), lambda m, n: (l, 0, n + col0 // tn)),
            pl.BlockSpec((tm, dk), lambda m, n: (m, 0)),
            pl.BlockSpec((tm, dk), lambda m, n: (m, 0)),
        ],
        out_specs=pl.BlockSpec((tm, tn), lambda m, n: (m, n)),
        out_shape=jax.ShapeDtypeStruct((mt, ncols), BF16),
        compiler_params=_params(("parallel", "arbitrary")),
        name="proj_b",
    )(xb, w_main, cos_t, sin_t)


def _proj_c_kernel(xb_ref, w_ref, g_ref, b_ref, u_ref, vn_ref):
    n = pl.program_id(1)
    acc = _gelu_tanh(jnp.dot(xb_ref[...], w_ref[...], preferred_element_type=F32))

    @pl.when(n == 0)
    def _():
        u_ref[...] = acc.astype(BF16)

    @pl.when(n == 1)
    def _():
        vn_ref[...] = _layer_norm(acc, g_ref[...], b_ref[...])


def _proj_c(xb, w_main, vln_g, vln_b, l, *, col0, wc, tm):
    mt, d = xb.shape
    assert col0 % wc == 0
    return pl.pallas_call(
        _proj_c_kernel,
        grid=(mt // tm, 2),
        in_specs=[
            pl.BlockSpec((tm, d), lambda m, n: (m, 0)),
            pl.BlockSpec((None, d, wc), lambda m, n: (l, 0, n + col0 // wc)),
            pl.BlockSpec((None, 1, wc), lambda m, n: (l, 0, 0)),
            pl.BlockSpec((None, 1, wc), lambda m, n: (l, 0, 0)),
        ],
        out_specs=[
            pl.BlockSpec((tm, wc), lambda m, n: (m, 0)),
            pl.BlockSpec((tm, wc), lambda m, n: (m, 0)),
        ],
        out_shape=[jax.ShapeDtypeStruct((mt, wc), BF16), jax.ShapeDtypeStruct((mt, wc), F32)],
        compiler_params=_params(("parallel", "arbitrary")),
        name="proj_c",
    )(xb, w_main, vln_g, vln_b)


def _proj_gate_kernel(xb_ref, w_ref, z_ref):
    acc = jnp.dot(xb_ref[...], w_ref[...], preferred_element_type=F32)
    z_ref[...] = jax.nn.sigmoid(acc).astype(BF16)


def _proj_gate(xb, w_main, l, *, col0, ncols, tm):
    mt, d = xb.shape
    tn = _pick_tile(math.gcd(ncols, col0), PROJ_TN, LANES)
    return pl.pallas_call(
        _proj_gate_kernel,
        grid=(mt // tm, ncols // tn),
        in_specs=[
            pl.BlockSpec((tm, d), lambda m, n: (m, 0)),
            pl.BlockSpec((None, d, tn), lambda m, n: (l, 0, n + col0 // tn)),
        ],
        out_specs=pl.BlockSpec((tm, tn), lambda m, n: (m, n)),
        out_shape=jax.ShapeDtypeStruct((mt, ncols), BF16),
        compiler_params=_params(("parallel", "arbitrary")),
        name="proj_gate",
    )(xb, w_main)


def _rope_kernel(pos_ref, cos_ref, sin_ref, *, half):
    lane = lax.broadcasted_iota(jnp.int32, pos_ref.shape, 1)
    j = jnp.where(lane >= half, lane - half, lane).astype(F32)
    inv = jnp.exp(j * (-math.log(ROPE_BASE) / half))
    ang = pos_ref[...] * inv
    cos_ref[...] = jnp.cos(ang)
    sin_ref[...] = jnp.where(lane >= half, 1.0, -1.0) * jnp.sin(ang)


def _rope_tables(pos_b, *, tm):
    mt, dk = pos_b.shape
    spec = pl.BlockSpec((tm, dk), lambda m: (m, 0))
    return pl.pallas_call(
        functools.partial(_rope_kernel, half=dk // 2),
        grid=(mt // tm,),
        in_specs=[spec],
        out_specs=[spec, spec],
        out_shape=[jax.ShapeDtypeStruct((mt, dk), F32)] * 2,
        compiler_params=_params(("parallel",)),
        name="rope_tables",
    )(pos_b)


def _cumsum_kernel(lf_ref, c_ref, carry_ref):
    @pl.when(pl.program_id(1) == 0)
    def _():
        carry_ref[...] = jnp.zeros_like(carry_ref)

    x = lf_ref[...]
    tc = x.shape[0]
    row = lax.broadcasted_iota(jnp.int32, (tc, tc), 0)
    col = lax.broadcasted_iota(jnp.int32, (tc, tc), 1)
    tri = jnp.where(col <= row, 1.0, 0.0).astype(BF16)
    c = carry_ref[0:1, :]
    for part in _split3(x):
        c = c + jnp.dot(tri, part, preferred_element_type=F32)
    c_ref[...] = c
    carry_ref[...] = jnp.broadcast_to(c[tc - 1:tc, :], carry_ref.shape)


def _cumsum(lf, *, batch, seq):
    tc = _pick_tile(seq, 512, 8)
    nt = seq // tc
    return pl.pallas_call(
        _cumsum_kernel,
        grid=(batch, nt),
        in_specs=[pl.BlockSpec((tc, LANES), lambda b, t: (b * nt + t, 0))],
        out_specs=pl.BlockSpec((tc, LANES), lambda b, t: (b * nt + t, 0)),
        out_shape=jax.ShapeDtypeStruct((batch * seq, LANES), F32),
        scratch_shapes=[pltpu.VMEM((8, LANES), F32)],
        compiler_params=_params(("parallel", "arbitrary")),
        name="logf_cumsum",
    )(lf)


def _fox_prompt_kernel(base_ref, q_ref, kb_ref, vb_ref, ck_ref, o_ref, m_ref, l_ref, acc_ref,
                       *, tq, hps, hd, scale):
    del base_ref
    i = pl.program_id(2)
    log2e = math.log2(math.e)

    qs = [(q_ref[:, hh * hd:(hh + 1) * hd] * (scale * log2e)).astype(BF16) for hh in range(hps)]
    m_ref[...] = jnp.full_like(m_ref, NEG_BIG)
    l_ref[...] = jnp.zeros_like(l_ref)
    acc_ref[...] = jnp.zeros_like(acc_ref)

    def step(j, masked):
        off = pl.multiple_of(j * tq, tq)
        for hh in range(hps):
            k = kb_ref[pl.ds(off, tq), hh * hd:(hh + 1) * hd]
            v = vb_ref[pl.ds(off, tq), hh * hd:(hh + 1) * hd]
            s = lax.dot_general(qs[hh], k, (((1,), (1,)), ((), ())), preferred_element_type=F32)
            s = s - ck_ref[hh, j] * log2e
            if masked:
                row = lax.broadcasted_iota(jnp.int32, s.shape, 0)
                col = lax.broadcasted_iota(jnp.int32, s.shape, 1)
                s = jnp.where(col <= row, s, NEG_BIG)
            m_prev = m_ref[hh]
            m_new = jnp.maximum(m_prev, jnp.max(s, axis=1, keepdims=True))
            p = jnp.exp2(s - jnp.tile(m_new, (1, tq // LANES)))
            alpha = jnp.exp2(m_prev - m_new)
            l_ref[hh] = alpha * l_ref[hh] + jnp.sum(p, axis=1, keepdims=True)
            acc_ref[hh] = alpha * acc_ref[hh] + jnp.dot(p.astype(BF16), v, preferred_element_type=F32)
            m_ref[hh] = m_new

    def body(j, carry):
        step(j, False)
        return carry

    lax.fori_loop(0, i, body, 0)
    step(i, True)
    for hh in range(hps):
        o_ref[:, hh * hd:(hh + 1) * hd] = (acc_ref[hh] / l_ref[hh]).astype(BF16)


def _fox_prompt(base, z_a, kv16, ck, *, batch, seq, heads, hd):
    assert hd == LANES
    tq = _pick_tile(seq, 512, LANES)
    nq = seq // tq
    hps = _pick_tile(heads, 4, 1)
    wb = hps * hd
    ng = heads // hps
    kern = functools.partial(_fox_prompt_kernel, tq=tq, hps=hps, hd=hd, scale=hd ** -0.5)
    return pl.pallas_call(
        kern,
        grid=(batch, ng, nq),
        in_specs=[
            pl.BlockSpec(memory_space=pl.ANY),
            pl.BlockSpec((tq, wb), lambda b, h, i: (b * nq + i, h)),
            pl.BlockSpec((seq, wb), lambda b, h, i: (b, h)),
            pl.BlockSpec((seq, wb), lambda b, h, i: (b, ng + h)),
            pl.BlockSpec((hps, nq, 1, tq), lambda b, h, i: (b * ng + h, 0, 0, 0)),
        ],
        out_specs=pl.BlockSpec((tq, wb), lambda b, h, i: (b * nq + i, h)),
        out_shape=jax.ShapeDtypeStruct(base.shape, base.dtype),
        input_output_aliases={0: 0},
        scratch_shapes=[
            pltpu.VMEM((hps, tq, LANES), F32),
            pltpu.VMEM((hps, tq, LANES), F32),
            pltpu.VMEM((hps, tq, hd), F32),
        ],
        compiler_params=_params(("parallel", "parallel", "arbitrary")),
        name="fox_prompt",
    )(base, z_a, kv16, kv16, ck)


def _fox_sample_kernel(pt_ref, qkv_ref, lfn_ref, ck_hbm, cv_hbm, lf_hbm, o_ref,
                       kbuf, vbuf, lbuf, sem, qs_ref, m_ref, l_ref, acc_ref, carry_ref,
                       *, layer, heads, hd, scale, n_pages, gp, dec_batch):
    b = pl.program_id(0)
    g = pl.program_id(1)
    ng = n_pages // gp
    step = b * ng + g
    slot = step % 2
    rows = lbuf.shape[2]

    def copies(bb, gg, sl):
        out = []
        for p in range(gp):
            pid = pt_ref[bb, n_pages - 1 - (gg * gp + p)]
            out.append(pltpu.make_async_copy(ck_hbm.at[layer, pid], kbuf.at[sl, p], sem.at[0, sl]))
            out.append(pltpu.make_async_copy(cv_hbm.at[layer, pid], vbuf.at[sl, p], sem.at[1, sl]))
            out.append(pltpu.make_async_copy(lf_hbm.at[layer, pid], lbuf.at[sl, p], sem.at[2, sl]))
        return out

    @pl.when(step == 0)
    def _():
        for c in copies(b, g, slot):
            c.start()

    @pl.when(step + 1 < dec_batch * ng)
    def _():
        wrap = g + 1 == ng
        for c in copies(jnp.where(wrap, b + 1, b), jnp.where(wrap, 0, g + 1), 1 - slot):
            c.start()

    lane = lax.broadcasted_iota(jnp.int32, (LANES, LANES), 1)
    src = lax.broadcasted_iota(jnp.int32, (LANES, LANES), 0)
    same_head = (src % heads) == (lane % heads)

    @pl.when(g == 0)
    def _():
        q = qkv_ref[0:heads, :] * scale
        qs_ref[...] = q.astype(BF16)
        m_ref[...] = jnp.sum(q * qkv_ref[heads:2 * heads, :], axis=1, keepdims=True)
        l_ref[...] = jnp.ones_like(l_ref)
        acc_ref[...] = qkv_ref[2 * heads:3 * heads, :]
        spread = jnp.where(src == lane % heads, 1.0, 0.0).astype(BF16)
        carry = jnp.zeros((1, LANES), F32)
        for part in _split3(lfn_ref[...]):
            carry = carry + jnp.dot(part, spread, preferred_element_type=F32)
        carry_ref[...] = carry

    for c in copies(b, g, slot):
        c.wait()

    later_w = jnp.where(jnp.logical_and(same_head, src > lane), 1.0, 0.0).astype(BF16)
    whole_w = jnp.where(same_head, 1.0, 0.0).astype(BF16)
    mix_w = jnp.concatenate([later_w, whole_w], axis=1)
    lf_all = lbuf[slot].reshape(gp * rows, LANES)
    sums = jnp.zeros((gp * rows, 2 * LANES), F32)
    for part in _split3(lf_all):
        sums = sums + jnp.dot(part, mix_w, preferred_element_type=F32)

    row_id = lax.broadcasted_iota(jnp.int32, (rows, LANES), 0)
    valid = (lax.broadcasted_iota(jnp.int32, (heads, LANES), 1) % heads
             == lax.broadcasted_iota(jnp.int32, (heads, LANES), 0))
    qs = qs_ref[...]
    carry = carry_ref[...]
    tiles = []
    for p in range(gp):
        within = sums[p * rows:(p + 1) * rows, :LANES]
        tot = sums[p * rows:(p + 1) * rows, LANES:]
        suf = tot
        sh = 1
        while sh < rows:
            suf = suf + jnp.where(row_id < rows - sh, pltpu.roll(suf, rows - sh, 0), 0.0)
            sh *= 2
        bias = carry + within + (suf - tot)
        carry = carry + suf[0:1, :]
        k2 = kbuf[slot, p].reshape(rows * LANES, hd).astype(BF16)
        s = lax.dot_general(qs, k2, (((1,), (1,)), ((), ())), preferred_element_type=F32)
        for r in range(rows):
            t = s[:, r * LANES:(r + 1) * LANES] + jnp.broadcast_to(bias[r:r + 1, :], (heads, LANES))
            tiles.append(jnp.where(valid, t, NEG_BIG))
    carry_ref[...] = carry

    m_tile = tiles[0]
    for t in tiles[1:]:
        m_tile = jnp.maximum(m_tile, t)
    m_prev = m_ref[...]
    m_new = jnp.maximum(m_prev, jnp.max(m_tile, axis=1, keepdims=True))
    alpha = jnp.exp(m_prev - m_new)
    probs = [jnp.exp(t - m_new) for t in tiles]
    p_sum = probs[0]
    for t in probs[1:]:
        p_sum = p_sum + t
    l_ref[...] = alpha * l_ref[...] + jnp.sum(p_sum, axis=1, keepdims=True)
    acc = alpha * acc_ref[...]
    for p in range(gp):
        pp = jnp.concatenate(probs[p * rows:(p + 1) * rows], axis=1).astype(BF16)
        v2 = vbuf[slot, p].reshape(rows * LANES, hd).astype(BF16)
        acc = acc + jnp.dot(pp, v2, preferred_element_type=F32)
    acc_ref[...] = acc
    m_ref[...] = m_new

    @pl.when(g == ng - 1)
    def _():
        o_ref[...] = acc_ref[...] / l_ref[...]


def _fox_sample(qkv_s, lf_s, cache_k, cache_v, cache_lf, page_table, l, *, heads, hd):
    dec_batch = qkv_s.shape[0]
    n_pages = page_table.shape[1]
    page = cache_k.shape[2]
    rows = cache_lf.shape[2]
    assert LANES % heads == 0 and rows * LANES == page * heads
    gp = _pick_tile(n_pages, 8, 1)
    kern = functools.partial(_fox_sample_kernel, layer=l, heads=heads, hd=hd, scale=hd ** -0.5,
                             n_pages=n_pages, gp=gp, dec_batch=dec_batch)
    grid_spec = pltpu.PrefetchScalarGridSpec(
        num_scalar_prefetch=1,
        grid=(dec_batch, n_pages // gp),
        in_specs=[
            pl.BlockSpec((None, 3 * heads, hd), lambda b, g, pt: (b, 0, 0)),
            pl.BlockSpec((None, 1, LANES), lambda b, g, pt: (b, 0, 0)),
            pl.BlockSpec(memory_space=pl.ANY),
            pl.BlockSpec(memory_space=pl.ANY),
            pl.BlockSpec(memory_space=pl.ANY),
        ],
        out_specs=pl.BlockSpec((None, heads, hd), lambda b, g, pt: (b, 0, 0)),
        scratch_shapes=[
            pltpu.VMEM((2, gp, page, heads, hd), F32),
            pltpu.VMEM((2, gp, page, heads, hd), F32),
            pltpu.VMEM((2, gp, rows, LANES), F32),
            pltpu.SemaphoreType.DMA((3, 2)),
            pltpu.VMEM((heads, hd), BF16),
            pltpu.VMEM((heads, 1), F32),
            pltpu.VMEM((heads, 1), F32),
            pltpu.VMEM((heads, hd), F32),
            pltpu.VMEM((1, LANES), F32),
        ],
    )
    return pl.pallas_call(
        kern,
        grid_spec=grid_spec,
        out_shape=jax.ShapeDtypeStruct((dec_batch, heads, hd), F32),
        compiler_params=_params(("arbitrary", "arbitrary")),
        name="fox_sample",
    )(page_table, qkv_s, lf_s, cache_k, cache_v, cache_lf)


def _log_gammas(n_heads):
    return [math.log1p(-(2.0 ** (-5.0 - h))) for h in range(n_heads)]


def _group_norm_gate(o, gate, g, b):
    mu = jnp.mean(o, axis=-1, keepdims=True)
    d = o - mu
    var = jnp.mean(d * d, axis=-1, keepdims=True)
    return gate * (d * lax.rsqrt(var + LN_EPS) * g + b)


def _ret_prompt_kernel(base_ref, q_ref, k_ref, v_ref, gate_ref, g_ref, b_ref, y_ref, st_ref,
                       *, heads, dk, dv, chunk):
    del base_ref

    @pl.when(pl.program_id(1) == 0)
    def _():
        st_ref[...] = jnp.zeros_like(st_ref)

    n_col = lax.broadcasted_iota(jnp.int32, (chunk, 1), 0).astype(F32)
    diff = (lax.broadcasted_iota(jnp.int32, (chunk, chunk), 0)
            - lax.broadcasted_iota(jnp.int32, (chunk, chunk), 1)).astype(F32)
    for h, lg in enumerate(_log_gammas(heads)):
        dmask = jnp.where(diff >= 0, jnp.exp(lg * jnp.maximum(diff, 0.0)), 0.0)
        dec_q = jnp.exp(lg * (n_col + 1.0))
        dec_k = jnp.exp(lg * (chunk - 1.0 - n_col))
        for c in range(q_ref.shape[0] // chunk):
            rows = slice(c * chunk, (c + 1) * chunk)
            q = q_ref[rows, h * dk:(h + 1) * dk]
            k = k_ref[rows, h * dk:(h + 1) * dk]
            v = v_ref[rows, h * dv:(h + 1) * dv]
            state = st_ref[h]
            a = lax.dot_general(q, k, (((1,), (1,)), ((), ())), preferred_element_type=F32) * dmask
            o = jnp.dot(a.astype(BF16), v, preferred_element_type=F32)
            o = o + jnp.dot(q, state.astype(BF16), preferred_element_type=F32) * dec_q
            kdec = k.astype(F32) * dec_k
            st_ref[h] = state * math.exp(lg * chunk) + jnp.dot(kdec.T.astype(BF16), v, preferred_element_type=F32)
            gate = gate_ref[rows, h * dv:(h + 1) * dv].astype(F32)
            y = _group_norm_gate(o, gate, g_ref[h:h + 1, :], b_ref[h:h + 1, :])
            y_ref[rows, h * dv:(h + 1) * dv] = y.astype(BF16)


def _ret_prompt(base, z_b, gn_g, gn_b, l, *, batch, seq, heads, dk, dv):
    chunk = LANES
    tr = _pick_tile(seq, 4 * chunk, chunk)
    nc = seq // tr
    wqk, wv = heads * dk, heads * dv
    assert wv % wqk == 0
    kern = functools.partial(_ret_prompt_kernel, heads=heads, dk=dk, dv=dv, chunk=chunk)
    return pl.pallas_call(
        kern,
        grid=(batch, nc),
        in_specs=[
            pl.BlockSpec(memory_space=pl.ANY),
            pl.BlockSpec((tr, wqk), lambda b, c: (b * nc + c, 0)),
            pl.BlockSpec((tr, wqk), lambda b, c: (b * nc + c, 1)),
            pl.BlockSpec((tr, wv), lambda b, c: (b * nc + c, 2 * wqk // wv)),
            pl.BlockSpec((tr, wv), lambda b, c: (b * nc + c, 2 * wqk // wv + 1)),
            pl.BlockSpec((None, heads, dv), lambda b, c: (l, 0, 0)),
            pl.BlockSpec((None, heads, dv), lambda b, c: (l, 0, 0)),
        ],
        out_specs=[
            pl.BlockSpec((tr, wv), lambda b, c: (b * nc + c, 0)),
            pl.BlockSpec((None, heads, dk, dv), lambda b, c: (b, 0, 0, 0)),
        ],
        out_shape=[
            jax.ShapeDtypeStruct(base.shape, base.dtype),
            jax.ShapeDtypeStruct((batch, heads, dk, dv), F32),
        ],
        input_output_aliases={0: 0},
        compiler_params=_params(("parallel", "arbitrary")),
        name="ret_prompt",
    )(base, z_b, z_b, z_b, z_b, gn_g, gn_b)


def _sample_mix_kernel(zb_ref, st_ref, g_ref, b_ref, u_ref, vn_ref, ws_ref, bs_ref, y_ref, yc_ref, nst_ref,
                       *, heads, dk, dv):
    b = pl.program_id(0)

    @pl.when(b == 0)
    def _():
        y_ref[...] = jnp.zeros_like(y_ref)
        yc_ref[...] = u_ref[...].astype(F32) * (ws_ref[...] * vn_ref[...] + bs_ref[...])

    wqk, wv = heads * dk, heads * dv
    blk = zb_ref[...].astype(F32)
    pick = lax.broadcasted_iota(jnp.int32, blk.shape, 0) == b
    row = jnp.sum(jnp.where(pick, blk, 0.0), axis=0, keepdims=True)
    eye =(lax.broadcasted_iota(jnp.int32, (dk, dk), 0) == lax.broadcasted_iota(jnp.int32, (dk, dk), 1))
    for h, lg in enumerate(_log_gammas(heads)):
        q = row[:, h * dk:(h + 1) * dk]
        k = row[:, wqk + h * dk:wqk + (h + 1) * dk]
        v = row[:, 2 * wqk + h * dv:2 * wqk + (h + 1) * dv]
        gate = row[:, 2 * wqk + wv + h * dv:2 * wqk + wv + (h + 1) * dv]
        qcol = jnp.sum(jnp.where(eye, jnp.broadcast_to(q, (dk, dk)), 0.0), axis=1, keepdims=True)
        kcol = jnp.sum(jnp.where(eye, jnp.broadcast_to(k, (dk, dk)), 0.0), axis=1, keepdims=True)
        state = st_ref[h]
        gamma = math.exp(lg)
        new_state = state * gamma + kcol * v
        nst_ref[h] = new_state
        o = jnp.sum(q * k, axis=1, keepdims=True) * v + jnp.sum(qcol * state, axis=0, keepdims=True) * gamma
        y = _group_norm_gate(o, gate, g_ref[h:h + 1, :], b_ref[h:h + 1, :])
        y_ref[pl.ds(b, 1), h * dv:(h + 1) * dv] = y


def _sample_mix(z_b, state_ret, gn_g, gn_b, u_c, vn, ws_row, bs_row, l, *, row0, dec_batch, heads, dk, dv):
    wv = heads * dv
    wc = u_c.shape[1]
    rb = row0 // SAMPLE_ROWS
    kern = functools.partial(_sample_mix_kernel, heads=heads, dk=dk, dv=dv)
    return pl.pallas_call(
        kern,
        grid=(dec_batch,),
        in_specs=[
            pl.BlockSpec((SAMPLE_ROWS, z_b.shape[1]), lambda b: (rb, 0)),
            pl.BlockSpec((None, None, heads, dk, dv), lambda b: (l, b, 0, 0, 0)),
            pl.BlockSpec((None, heads, dv), lambda b: (l, 0, 0)),
            pl.BlockSpec((None, heads, dv), lambda b: (l, 0, 0)),
            pl.BlockSpec((SAMPLE_ROWS, wc), lambda b: (rb, 0)),
            pl.BlockSpec((SAMPLE_ROWS, wc), lambda b: (rb, 0)),
            pl.BlockSpec((None, 1, wc), lambda b: (l, 0, 0)),
            pl.BlockSpec((None, 1, wc), lambda b: (l, 0, 0)),
        ],
        out_specs=[
            pl.BlockSpec((SAMPLE_ROWS, wv), lambda b: (0, 0)),
            pl.BlockSpec((SAMPLE_ROWS, wc), lambda b: (0, 0)),
            pl.BlockSpec((None, heads, dk, dv), lambda b: (b, 0, 0, 0)),
        ],
        out_shape=[
            jax.ShapeDtypeStruct((SAMPLE_ROWS, wv), F32),
            jax.ShapeDtypeStruct((SAMPLE_ROWS, wc), F32),
            jax.ShapeDtypeStruct((dec_batch, heads, dk, dv), F32),
        ],
        compiler_params=_params(("arbitrary",)),
        name="sample_mix",
    )(z_b, state_ret, gn_g, gn_b, u_c, vn, ws_row, bs_row)


def _chunk_mix_kernel(base_ref, vn_ref, u_ref, ws_ref, bst_ref, y_ref, *, groups, cg):
    del base_ref
    chunk = ws_ref.shape[1]
    lower = (lax.broadcasted_iota(jnp.int32, (chunk, chunk), 1)
             <= lax.broadcasted_iota(jnp.int32, (chunk, chunk), 0))
    bst = bst_ref[...]
    for g in range(groups):
        w = jnp.where(lower, ws_ref[g], 0.0).astype(BF16)
        cols = slice(g * cg, (g + 1) * cg)
        for c in range(vn_ref.shape[0] // chunk):
            rows = slice(c * chunk, (c + 1) * chunk)
            s = jnp.dot(w, vn_ref[rows, cols].astype(BF16), preferred_element_type=F32) + bst[:, g:g + 1]
            y_ref[rows, cols] = (u_ref[rows, cols].astype(F32) * s).astype(BF16)


def _chunk_mix(base, vn, u_c, w_spatial, bs_t, l, *, rows, groups, cg):
    chunk = w_spatial.shape[2]
    wc = groups * cg
    tr = _pick_tile(rows, 4 * chunk, chunk)
    kern = functools.partial(_chunk_mix_kernel, groups=groups, cg=cg)
    return pl.pallas_call(
        kern,
        grid=(rows // tr,),
        in_specs=[
            pl.BlockSpec(memory_space=pl.ANY),
            pl.BlockSpec((tr, wc), lambda n: (n, 0)),
            pl.BlockSpec((tr, wc), lambda n: (n, 0)),
            pl.BlockSpec((None, groups, chunk, chunk), lambda n: (l, 0, 0, 0)),
            pl.BlockSpec((None, chunk, groups), lambda n: (l, 0, 0)),
        ],
        out_specs=pl.BlockSpec((tr, wc), lambda n: (n, 0)),
        out_shape=jax.ShapeDtypeStruct(base.shape, base.dtype),
        input_output_aliases={0: 0},
        compiler_params=_params(("parallel",)),
        name="chunk_mix",
    )(base, vn, u_c, w_spatial, bs_t)


def _merge_kernel(oa_ref, yb_ref, yc_ref, ga_ref, gb_ref, gc_ref, wa_ref, wb_ref, wc_ref, y_ref):
    ya = jnp.dot(oa_ref[...], wa_ref[...], preferred_element_type=F32)
    yb = jnp.dot(yb_ref[...], wb_ref[...], preferred_element_type=F32)
    yc = jnp.dot(yc_ref[...], wc_ref[...], preferred_element_type=F32)
    y = ga_ref[...].astype(F32) * ya + gb_ref[...].astype(F32) * yb + gc_ref[...].astype(F32) * yc
    y_ref[...] = y.astype(BF16)


def _merge(o_a, y_b, y_c, gates, w_up_a, w_up_b, w_up_c, l, *, tm):
    mt = o_a.shape[0]
    d = w_up_a.shape[2]
    tn = _pick_tile(d, PROJ_TN, LANES)
    nn = d // tn

    def act(a):
        return pl.BlockSpec((tm, a.shape[1]), lambda m, n: (m, 0))

    def wgt(w):
        return pl.BlockSpec((None, w.shape[1], tn), lambda m, n: (l, 0, n))

    return pl.pallas_call(
        _merge_kernel,
        grid=(mt // tm, nn),
        in_specs=[
            act(o_a), act(y_b), act(y_c),
            pl.BlockSpec((tm, tn), lambda m, n: (m, n)),
            pl.BlockSpec((tm, tn), lambda m, n: (m, n + nn)),
            pl.BlockSpec((tm, tn), lambda m, n: (m, n + 2 * nn)),
            wgt(w_up_a), wgt(w_up_b), wgt(w_up_c),
        ],
        out_specs=pl.BlockSpec((tm, tn), lambda m, n: (m, n)),
        out_shape=jax.ShapeDtypeStruct((mt, d), BF16),
        compiler_params=_params(("parallel", "arbitrary")),
        name="merge",
    )(o_a, y_b, y_c, gates, gates, gates, w_up_a, w_up_b, w_up_c)


def _out_kernel(x_ref, y_ref, w_ref, g_ref, b_ref, o_ref, *, alpha):
    sub = jnp.dot(y_ref[...], w_ref[...], preferred_element_type=F32)
    o_ref[...] = _layer_norm(alpha * x_ref[...] + sub, g_ref[...], b_ref[...])


def _out_proj(x, y, w_out, ln_g, ln_b, l, ln_idx, *, alpha):
    mt, d = x.shape
    tm = _pick_tile(mt, 512, SAMPLE_ROWS)
    row = pl.BlockSpec((tm, d), lambda m: (m, 0))
    return pl.pallas_call(
        functools.partial(_out_kernel, alpha=alpha),
        grid=(mt // tm,),
        in_specs=[
            row, row,
            pl.BlockSpec((None, d, d), lambda m: (l, 0, 0)),
            pl.BlockSpec((None, 1, d), lambda m: (ln_idx, 0, 0)),
            pl.BlockSpec((None, 1, d), lambda m: (ln_idx, 0, 0)),
        ],
        out_specs=row,
        out_shape=jax.ShapeDtypeStruct((mt, d), F32),
        compiler_params=_params(("parallel",)),
        name="out_proj",
    )(x, y, w_out, ln_g, ln_b)


def _sample_rows_base(prev, sample_rows, n_prompt):
    pad = SAMPLE_ROWS - sample_rows.shape[0]
    block = jnp.pad(sample_rows.astype(BF16), ((0, pad), (0, 0)))
    if prev is None:
        return jnp.pad(block, ((n_prompt, 0), (0, 0)))
    return lax.dynamic_update_slice(prev, block, (n_prompt, 0))


def kernel(x_prompt, x_sample, cache_k, cache_v, cache_logf, state_ret, page_table, w_in, b_forget, w_up_a, w_up_b, w_up_c, w_out, ret_gn_g, ret_gn_b, vln_g, vln_b, w_spatial, b_spatial, ffa_gu, ffa_down, ffb_gu, ffb_down, ln_g, ln_b):
    batch, seq, d = x_prompt.shape
    dec_batch, dec_seq, _ = x_sample.shape
    depth = w_in.shape[0]
    heads_a, hd = cache_k.shape[3], cache_k.shape[4]
    n_pool, page = cache_k.shape[1], cache_k.shape[2]
    heads_b, dk, dv = state_ret.shape[2], state_ret.shape[3], state_ret.shape[4]
    groups, chunk = w_spatial.shape[1], w_spatial.shape[2]
    wa, wqk, wv, wc = heads_a * hd, heads_b * dk, heads_b * dv, vln_g.shape[1]
    cg = wc // groups
    assert dec_seq == 1 and dec_batch <= SAMPLE_ROWS and heads_a <= LANES
    assert seq % chunk == 0 and chunk == LANES and (batch * seq) % SAMPLE_ROWS == 0
    assert w_in.shape[2] == 3 * wa + heads_a + 2 * wqk + 2 * wv + 2 * wc + 3 * d
    past_len = page_table.shape[1] * page
    alpha = (2 * depth) ** 0.25

    n_prompt = batch * seq
    mt = n_prompt + SAMPLE_ROWS
    tm = _pick_tile(mt, 1024, SAMPLE_ROWS)

    w_abc = w_in.astype(BF16)
    w_rest = w_abc[:, :, 3 * wa + heads_a:]
    w_f =jnp.pad(w_in[:, :, 3 * wa:3 * wa + heads_a], ((0, 0), (0, 0), (0, LANES - heads_a))).astype(BF16)
    b_f = jnp.pad(b_forget, ((0, 0), (0, LANES - heads_a)))[:, None, :]
    ffa_gu16, ffa_down16 = ffa_gu.astype(BF16), ffa_down.astype(BF16)
    ffb_gu16, ffb_down16 = ffb_gu.astype(BF16), ffb_down.astype(BF16)
    w_up_a16, w_up_b16, w_up_c16 = w_up_a.astype(BF16), w_up_b.astype(BF16), w_up_c.astype(BF16)
    w_out16 = w_out.astype(BF16)
    ln_g3 = ln_g.reshape(depth * 3, 1, d)
    ln_b3 = ln_b.reshape(depth * 3, 1, d)
    vln_g3, vln_b3 = vln_g[:, None, :], vln_b[:, None, :]
    bs_t = jnp.swapaxes(b_spatial, 1, 2)
    ws_row = jnp.repeat(w_spatial[:, :, 0, 0], cg, axis=1)[:, None, :]
    bs_row = jnp.repeat(b_spatial[:, :, 0], cg, axis=1)[:, None, :]
    cache_lf = cache_logf.reshape(depth, n_pool, page * heads_a // LANES, LANES)

    pos = jnp.concatenate([jnp.tile(jnp.arange(seq, dtype=F32), batch), jnp.full((SAMPLE_ROWS,), past_len, F32)])
    cos_t, sin_t = _rope_tables(jnp.broadcast_to(pos[:, None], (mt, dk)), tm=tm)

    x = jnp.concatenate([x_prompt.reshape(n_prompt, d), x_sample.reshape(dec_batch, d),
                         jnp.zeros((SAMPLE_ROWS - dec_batch, d), F32)], axis=0)

    col_c = 2 * wqk + 2 * wv
    col_g = col_c + 2 * wc
    s0, s1 = n_prompt, n_prompt + dec_batch
    outs = [[] for _ in range(9)]
    o_a = y_b = y_c = None
    for l in range(depth):
        x = _ffn(x, ffa_gu16, ffa_down16, ln_g3, ln_b3, l, 3 * l, alpha=alpha, tm=tm)

        z_a, lf, xb, kv16 = _proj_a(x, w_abc, w_f, b_f, l, wa=wa, tm=tm)
        z_b = _proj_b(xb, w_rest, cos_t, sin_t, l, col0=0, wqk=wqk, wv=wv, dk=dk, tm=tm)
        u_c, vn = _proj_c(xb, w_rest, vln_g3, vln_b3, l, col0=col_c, wc=wc, tm=tm)
        gates = _proj_gate(xb, w_rest, l, col0=col_g, ncols=3 * d, tm=tm)

        c = _cumsum(lf, batch=batch, seq=seq)
        tq = _pick_tile(seq, 512, LANES)
        ck = jnp.swapaxes(c[:, :heads_a].reshape(batch, seq, heads_a), 1, 2).reshape(batch * heads_a, seq // tq, 1, tq)
        o_a_s = _fox_sample(z_a[s0:s1].reshape(dec_batch, 3 * heads_a, hd), lf[s0:s1, None, :],
                            cache_k, cache_v, cache_lf, page_table, l, heads=heads_a, hd=hd)
        y_b_s, y_c_s, st_s = _sample_mix(z_b, state_ret, ret_gn_g, ret_gn_b, u_c, vn, ws_row, bs_row, l,
                                         row0=n_prompt, dec_batch=dec_batch, heads=heads_b, dk=dk, dv=dv)

        o_a = _fox_prompt(_sample_rows_base(o_a, o_a_s.reshape(dec_batch, wa), n_prompt), z_a, kv16, ck,
                          batch=batch, seq=seq, heads=heads_a, hd=hd)
        y_b, st_p = _ret_prompt(_sample_rows_base(y_b, y_b_s[:dec_batch], n_prompt), z_b, ret_gn_g, ret_gn_b, l,
                                batch=batch, seq=seq, heads=heads_b, dk=dk, dv=dv)
        y_c = _chunk_mix(_sample_rows_base(y_c, y_c_s[:dec_batch], n_prompt), vn, u_c, w_spatial, bs_t, l,
                         rows=n_prompt, groups=groups, cg=cg)

        y = _merge(o_a, y_b, y_c, gates, w_up_a16, w_up_b16, w_up_c16, l, tm=tm)
        x = _out_proj(x, y, w_out16, ln_g3, ln_b3, l, 3 * l + 1, alpha=alpha)
        x = _ffn(x, ffb_gu16, ffb_down16, ln_g3, ln_b3, l, 3 * l + 2, alpha=alpha, tm=tm)

        outs[0].append(z_a[:n_prompt, wa:2 * wa].reshape(batch, seq, heads_a, hd))
        outs[1].append(z_a[:n_prompt, 2 * wa:].reshape(batch, seq, heads_a, hd))
        outs[2].append(lf[:n_prompt, :heads_a].reshape(batch, seq, heads_a))
        outs[3].append(st_p)
        outs[4].append(z_a[s0:s1, wa:2 * wa].reshape(dec_batch, 1, heads_a, hd))
        outs[5].append(z_a[s0:s1, 2 * wa:].reshape(dec_batch, 1, heads_a, hd))
        outs[6].append(lf[s0:s1, :heads_a].reshape(dec_batch, 1, heads_a))
        outs[7].append(st_s)
        outs[8].append(vn[s0:s1].reshape(dec_batch, 1, groups, cg))

    y_prompt = x[:n_prompt].reshape(batch, seq, d)
    y_sample = x[n_prompt:n_prompt + dec_batch].reshape(dec_batch, 1, d)
    return (y_prompt, y_sample) + tuple(jnp.stack(o) for o in outs)
```
